```python
import math
import jax, jax.numpy as jnp
from jax import lax
import numpy as np

D_MODEL = 2048
BATCH = 4
SEQ = 8192
DEPTH = 2
DEC_BATCH = 2
DEC_SEQ = 16384
PAST_LEN = 128

N_MIXERS = 2
N_MLSTM_LAYERS = (DEPTH + 1) // 2
N_SGU_LAYERS = DEPTH // 2

MLSTM_HEADS = 4
QK_HEAD = D_MODEL // (2 * MLSTM_HEADS)
V_HEAD = D_MODEL // MLSTM_HEADS
QK_DIM = MLSTM_HEADS * QK_HEAD
V_DIM = MLSTM_HEADS * V_HEAD
N_GATE_COLS = 4 * MLSTM_HEADS
MLSTM_IN = 2 * QK_DIM + 2 * V_DIM + N_GATE_COLS
MLSTM_CHUNK = 128
GATE_SOFTCAP = 15.0

SGU_DIM = D_MODEL
SGU_GROUPS = 8
SGU_GROUP_DIM = SGU_DIM // SGU_GROUPS
SGU_CHUNK = 128

N_EXPERT_GROUPS = 4
EXPERTS_PER_GROUP = 8
N_EXPERTS = N_EXPERT_GROUPS * EXPERTS_PER_GROUP
TOP_K = 2
D_EXPERT = D_MODEL // 4
MOE_BLOCK = 128

EPS = 1e-6

kernel_name = "bidir_mlstm_sgu_hier_moe_encoder"


def rmsnorm(x, g):
    x32 = x.astype(jnp.float32)
    y = x32 * lax.rsqrt(jnp.mean(x32 * x32, axis=-1, keepdims=True) + EPS) * g.astype(jnp.float32)
    return y.astype(x.dtype)


def soft_cap(t):
    return GATE_SOFTCAP * jnp.tanh(t / GATE_SOFTCAP)


def mlstm_scan(q, k, v, ig, fg):
    B, H, S, DK = q.shape
    DV = v.shape[-1]
    L = MLSTM_CHUNK
    NC = S // L
    logf = jax.nn.log_sigmoid(fg)
    qc = q.reshape(B, H, NC, L, DK)
    kc = k.reshape(B, H, NC, L, DK)
    vc = v.reshape(B, H, NC, L, DV)
    ic = ig.reshape(B, H, NC, L)
    b = jnp.cumsum(logf.reshape(B, H, NC, L), axis=-1)
    g = b[..., -1]
    a = g[..., None] - b + ic

    def step(carry, xs):
        C, n, m = carry
        q_, k_, v_, a_, g_ = xs
        num = jnp.einsum('bhld,bhde->bhle', q_, C)
        den = jnp.einsum('bhld,bhd->bhl', q_, n)
        m_new = jnp.maximum(g_ + m, jnp.max(a_, axis=-1))
        decay = jnp.exp(g_ + m - m_new)
        kw = k_ * jnp.exp(a_ - m_new[..., None])[..., None]
        C = decay[..., None, None] * C + jnp.einsum('bhld,bhle->bhde', kw, v_)
        n = decay[..., None] * n + jnp.sum(kw, axis=2)
        return (C, n, m_new), (num, den, m)

    init = (jnp.zeros((B, H, DK, DV), jnp.float32),
            jnp.zeros((B, H, DK), jnp.float32),
            jnp.zeros((B, H), jnp.float32))
    xs = (jnp.moveaxis(qc, 2, 0), jnp.moveaxis(kc, 2, 0), jnp.moveaxis(vc, 2, 0),
          jnp.moveaxis(a, 2, 0), jnp.moveaxis(g, 2, 0))
    _, (num_i, den_i, m_prev) = lax.scan(step, init, xs)
    num_i = jnp.moveaxis(num_i, 0, 2)
    den_i = jnp.moveaxis(den_i, 0, 2)
    m_prev = jnp.moveaxis(m_prev, 0, 2)

    log_inter = b + m_prev[..., None]
    Dlog = b[..., :, None] - b[..., None, :] + ic[..., None, :]
    mask = jnp.tril(jnp.ones((L, L), dtype=bool))
    Dlog = jnp.where(mask, Dlog, -jnp.inf)
    m_t = jnp.maximum(log_inter, jnp.max(Dlog, axis=-1))
    Smat = jnp.einsum('bhctd,bhcsd->bhcts', qc, kc) * jnp.exp(Dlog - m_t[..., None])
    inter_scale = jnp.exp(log_inter - m_t)
    num = inter_scale[..., None] * num_i + jnp.einsum('bhcts,bhcse->bhcte', Smat, vc)
    den = inter_scale * den_i + jnp.sum(Smat, axis=-1)
    h = num / jnp.maximum(jnp.abs(den), jnp.exp(-m_t))[..., None]
    return h.reshape(B, H, S, DV)


def mlstm_mixer(x, w_in, b_gates, g_hnorm, w_out):
    B, S, _ = x.shape
    H = MLSTM_HEADS
    z = x @ w_in
    q = z[..., :QK_DIM]
    k = z[..., QK_DIM:2 * QK_DIM]
    v = z[..., 2 * QK_DIM:2 * QK_DIM + V_DIM]
    o = z[..., 2 * QK_DIM + V_DIM:2 * QK_DIM + 2 * V_DIM]
    gates = soft_cap(z[..., 2 * QK_DIM + 2 * V_DIM:].astype(jnp.float32) + b_gates.astype(jnp.float32))
    i_f, f_f, i_b, f_b = gates.reshape(B, S, 4, H).transpose(2, 0, 3, 1)

    def to_heads(t, dh):
        return t.reshape(B, S, H, dh).transpose(0, 2, 1, 3).astype(jnp.float32)

    qh = to_heads(q, QK_HEAD) * (QK_HEAD ** -0.5)
    kh = to_heads(k, QK_HEAD)
    vh = to_heads(v, V_HEAD)
    h_f = mlstm_scan(qh, kh, vh, i_f, f_f)
    flip = lambda t: jnp.flip(t, axis=2)
    h_b = flip(mlstm_scan(flip(qh), flip(kh), flip(vh), flip(i_b), flip(f_b)))
    h = h_f + h_b
    h = h * lax.rsqrt(jnp.mean(h * h, axis=-1, keepdims=True) + EPS)
    h = h.transpose(0, 2, 1, 3).reshape(B, S, V_DIM) * g_hnorm.astype(jnp.float32)
    h = h * jax.nn.sigmoid(o.astype(jnp.float32))
    return h.astype(x.dtype) @ w_out


def sgu_mixer(x, w_in, g_v, w_s, b_s, w_out):
    B, S, _ = x.shape
    NC = S // SGU_CHUNK
    z = jax.nn.gelu(x @ w_in)
    u = z[..., :SGU_DIM]
    v = rmsnorm(z[..., SGU_DIM:], g_v)
    v = v.reshape(B, NC, SGU_CHUNK, SGU_GROUPS, SGU_GROUP_DIM)
    v = jnp.einsum('gts,bcsgd->bctgd', w_s, v) + b_s.T[None, None, :, :, None]
    v = v.reshape(B, S, SGU_DIM)
    return (u * v) @ w_out


def hier_moe(x, w_grp, b_grp, w_exp, b_exp, w_gate, w_up, w_down):
    B, S, D = x.shape
    T = B * S
    xt = x.reshape(T, D)
    grp_logits = (xt @ w_grp).astype(jnp.float32) + b_grp.astype(jnp.float32)
    grp_prob = jax.nn.softmax(grp_logits, axis=-1)
    grp = jnp.argmax(grp_logits, axis=-1)
    tok = jnp.arange(T, dtype=jnp.int32)
    grp_p = grp_prob[tok, grp]
    exp_logits = ((xt @ w_exp).astype(jnp.float32) + b_exp.astype(jnp.float32)).reshape(T, N_EXPERT_GROUPS, EXPERTS_PER_GROUP)
    sub_prob = jax.nn.softmax(exp_logits[tok, grp], axis=-1)
    top_p, top_i = lax.top_k(sub_prob, TOP_K)
    gates = grp_p[:, None] * (top_p / jnp.sum(top_p, axis=-1, keepdims=True))
    expert = (grp[:, None] * EXPERTS_PER_GROUP + top_i).astype(jnp.int32)

    TK = T * TOP_K
    flat_e = expert.reshape(-1)
    flat_t = jnp.repeat(tok, TOP_K)
    flat_g = gates.reshape(-1)
    order = jnp.argsort(flat_e)
    se = flat_e[order]
    counts = jnp.bincount(flat_e, length=N_EXPERTS)
    starts = jnp.cumsum(counts) - counts
    pcounts = (counts + MOE_BLOCK - 1) // MOE_BLOCK * MOE_BLOCK
    pends = jnp.cumsum(pcounts)
    pstarts = pends - pcounts
    dest = pstarts[se] + jnp.arange(TK, dtype=jnp.int32) - starts[se]
    n_blk = -(-TK // MOE_BLOCK) + N_EXPERTS
    P = n_blk * MOE_BLOCK
    buf_tok = jnp.zeros((P,), jnp.int32).at[dest].set(flat_t[order])
    buf_gate = jnp.zeros((P,), jnp.float32).at[dest].set(flat_g[order])
    blk_exp = jnp.minimum(jnp.searchsorted(pends, jnp.arange(n_blk, dtype=jnp.int32) * MOE_BLOCK, side='right'),
                          N_EXPERTS - 1).astype(jnp.int32)
    xb = xt[buf_tok].reshape(n_blk, MOE_BLOCK, D)

    def expert_block(args):
        xblk, e = args
        hid = jax.nn.silu(xblk @ w_gate[e]) * (xblk @ w_up[e])
        return hid @ w_down[e]

    yb = lax.map(expert_block, (xb, blk_exp))
    y = jnp.zeros((T, D), jnp.float32).at[buf_tok].add(yb.reshape(P, D).astype(jnp.float32) * buf_gate[:, None])
    return y.astype(x.dtype).reshape(B, S, D)


def trunk(x, norm_mix, norm_ffn, norm_final, mlstm_w_in, mlstm_b_gates, mlstm_g_hnorm, mlstm_w_out,
          sgu_w_in, sgu_g_v, sgu_w_s, sgu_b_s, sgu_w_out,
          moe_w_grp, moe_b_grp, moe_w_exp, moe_b_exp, moe_w_gate, moe_w_up, moe_w_down):
    for i in range(DEPTH):
        j = i // N_MIXERS
        h = rmsnorm(x, norm_mix[i])
        if i % N_MIXERS == 0:
            x = x + mlstm_mixer(h, mlstm_w_in[j], mlstm_b_gates[j], mlstm_g_hnorm[j], mlstm_w_out[j])
        else:
            x = x + sgu_mixer(h, sgu_w_in[j], sgu_g_v[j], sgu_w_s[j], sgu_b_s[j], sgu_w_out[j])
        h = rmsnorm(x, norm_ffn[i])
        x = x + hier_moe(h, moe_w_grp[i], moe_b_grp[i], moe_w_exp[i], moe_b_exp[i],
                         moe_w_gate[i], moe_w_up[i], moe_w_down[i])
    return rmsnorm(x, norm_final)


def setup_inputs(seed: int = 0) -> dict:
    key = jax.random.key(seed)
    ks = jax.random.split(key, 24)
    f32 = jnp.float32

    def nrm(k, shape, fan_in):
        return jax.random.normal(k, shape, f32) * (fan_in ** -0.5)

    gate_offset = jnp.array([0.0, 3.0, 0.0, 3.0], f32)[None, :, None]
    b_gates = (gate_offset + 0.5 * jax.random.normal(ks[5], (N_MLSTM_LAYERS, 4, MLSTM_HEADS), f32)).reshape(N_MLSTM_LAYERS, N_GATE_COLS)
    return {
        "x_prompt": jax.random.normal(ks[0], (BATCH, SEQ, D_MODEL), f32),
        "x_sample": jax.random.normal(ks[1], (DEC_BATCH, DEC_SEQ, D_MODEL), f32),
        "norm_mix": 1.0 + 0.05 * jax.random.normal(ks[2], (DEPTH, D_MODEL), f32),
        "norm_ffn": 1.0 + 0.05 * jax.random.normal(ks[3], (DEPTH, D_MODEL), f32),
        "norm_final": 1.0 + 0.05 * jax.random.normal(ks[4], (D_MODEL,), f32),
        "mlstm_w_in": nrm(ks[6], (N_MLSTM_LAYERS, D_MODEL, MLSTM_IN), D_MODEL),
        "mlstm_b_gates": b_gates,
        "mlstm_g_hnorm": 1.0 + 0.05 * jax.random.normal(ks[7], (N_MLSTM_LAYERS, V_DIM), f32),
        "mlstm_w_out": nrm(ks[8], (N_MLSTM_LAYERS, V_DIM, D_MODEL), V_DIM),
        "sgu_w_in": nrm(ks[9], (N_SGU_LAYERS, D_MODEL, 2 * SGU_DIM), D_MODEL),
        "sgu_g_v": 1.0 + 0.05 * jax.random.normal(ks[10], (N_SGU_LAYERS, SGU_DIM), f32),
        "sgu_w_s": nrm(ks[11], (N_SGU_LAYERS, SGU_GROUPS, SGU_CHUNK, SGU_CHUNK), SGU_CHUNK),
        "sgu_b_s": 1.0 + 0.1 * jax.random.normal(ks[12], (N_SGU_LAYERS, SGU_GROUPS, SGU_CHUNK), f32),
        "sgu_w_out": nrm(ks[13], (N_SGU_LAYERS, SGU_DIM, D_MODEL), SGU_DIM),
        "moe_w_grp": nrm(ks[14], (DEPTH, D_MODEL, N_EXPERT_GROUPS), D_MODEL),
        "moe_b_grp": 0.01 * jax.random.normal(ks[15], (DEPTH, N_EXPERT_GROUPS), f32),
        "moe_w_exp": nrm(ks[16], (DEPTH, D_MODEL, N_EXPERTS), D_MODEL),
        "moe_b_exp": 0.01 * jax.random.normal(ks[17], (DEPTH, N_EXPERTS), f32),
        "moe_w_gate": nrm(ks[18], (DEPTH, N_EXPERTS, D_MODEL, D_EXPERT), D_MODEL),
        "moe_w_up": nrm(ks[19], (DEPTH, N_EXPERTS, D_MODEL, D_EXPERT), D_MODEL),
        "moe_w_down": nrm(ks[20], (DEPTH, N_EXPERTS, D_EXPERT, D_MODEL), D_EXPERT),
    }


def reference(x_prompt, x_sample, norm_mix, norm_ffn, norm_final, mlstm_w_in, mlstm_b_gates, mlstm_g_hnorm,
              mlstm_w_out, sgu_w_in, sgu_g_v, sgu_w_s, sgu_b_s, sgu_w_out,
              moe_w_grp, moe_b_grp, moe_w_exp, moe_b_exp, moe_w_gate, moe_w_up, moe_w_down):
    params = (norm_mix, norm_ffn, norm_final, mlstm_w_in, mlstm_b_gates, mlstm_g_hnorm, mlstm_w_out,
              sgu_w_in, sgu_g_v, sgu_w_s, sgu_b_s, sgu_w_out,
              moe_w_grp, moe_b_grp, moe_w_exp, moe_b_exp, moe_w_gate, moe_w_up, moe_w_down)
    y_prompt = trunk(x_prompt, *params)
    y_sample = trunk(x_sample, *params)
    return (y_prompt, y_sample)
```

```python
import functools

import jax
import jax.numpy as jnp
from jax import lax
from jax.experimental import pallas as pl
from jax.experimental.pallas import tpu as pltpu

F32 = jnp.float32
BF16 = jnp.bfloat16
U32 = jnp.uint32
I32 = jnp.int32

EPS = 1e-6
MLSTM_HEADS = 4
MLSTM_CHUNK = 128
GATE_SOFTCAP = 15.0
N_GATE_LANES = 128
SGU_CHUNK = 128
TOP_K = 2
ROW_CHUNK = 128
MOE_BLOCK = 256
LANES = 128
VMEM_LIMIT_BYTES = 56 * 1024 * 1024


def _cparams(*sem):
    return pltpu.CompilerParams(dimension_semantics=sem, vmem_limit_bytes=VMEM_LIMIT_BYTES)


def _pick(n, prefs):
    for p in prefs:
        if n % p == 0:
            return p
    return n


def _rmsnorm_rows(x_ref, g_ref, out_ref):
    rows = x_ref.shape[0]
    g = g_ref[...]

    def body(i, c):
        r = pl.multiple_of(i * ROW_CHUNK, ROW_CHUNK)
        x = x_ref[pl.ds(r, ROW_CHUNK), :]
        ms = jnp.mean(x * x, axis=-1, keepdims=True)
        out_ref[pl.ds(r, ROW_CHUNK), :] = (x * lax.rsqrt(ms + EPS) * g).astype(out_ref.dtype)
        return c

    lax.fori_loop(0, rows // ROW_CHUNK, body, 0)


def _sigmoid(x):
    return 1.0 / (1.0 + jnp.exp(-x))


def _log_sigmoid(x):
    return jnp.minimum(x, 0.0) - jnp.log1p(jnp.exp(-jnp.abs(x)))


def _soft_cap(t):
    return GATE_SOFTCAP * jnp.tanh(t / GATE_SOFTCAP)


def _inproj_mlstm_kernel(x_ref, g_ref, w_ref, wg_ref, z_ref, gate_ref, xn_ref):
    @pl.when(pl.program_id(1) == 0)
    def _():
        _rmsnorm_rows(x_ref, g_ref, xn_ref)
        gate_ref[...] = jnp.dot(xn_ref[...], wg_ref[...], preferred_element_type=F32)

    z_ref[...] = jnp.dot(xn_ref[...], w_ref[...], preferred_element_type=F32).astype(z_ref.dtype)


def _inproj_mlstm(x, g, w, wg):
    t, d = x.shape
    n = w.shape[1]
    tm = _pick(t, (1024, 512, 256, 128))
    tn = _pick(n, (512, 256, 128))
    return pl.pallas_call(
        _inproj_mlstm_kernel,
        grid=(t // tm, n // tn),
        in_specs=[
            pl.BlockSpec((tm, d), lambda i, j: (i, 0)),
            pl.BlockSpec((1, d), lambda i, j: (0, 0)),
            pl.BlockSpec((d, tn), lambda i, j: (0, j)),
            pl.BlockSpec((d, N_GATE_LANES), lambda i, j: (0, 0)),
        ],
        out_specs=[
            pl.BlockSpec((tm, tn), lambda i, j: (i, j)),
            pl.BlockSpec((tm, N_GATE_LANES), lambda i, j: (i, 0)),
        ],
        out_shape=[
            jax.ShapeDtypeStruct((t, n), BF16),
            jax.ShapeDtypeStruct((t, N_GATE_LANES), F32),
        ],
        scratch_shapes=[pltpu.VMEM((tm, d), BF16)],
        compiler_params=_cparams("parallel", "arbitrary"),
        name="inproj_mlstm",
    )(x, g, w, wg)


def _mlstm_kernel(*refs, reverse, fuse, heads, dk, dv):
    if fuse:
        (q_ref, k_ref, v_ref, gcol_ref, grow_ref, brow_ref, bcol_ref, o_ref, hf_ref, gh_ref,
         out_ref, c_ref, n_ref, m_ref) = refs
    else:
        (q_ref, k_ref, v_ref, gcol_ref, grow_ref, brow_ref, bcol_ref,
         out_ref, c_ref, n_ref, m_ref) = refs
    L = MLSTM_CHUNK

    @pl.when(pl.program_id(1) == 0)
    def _():
        c_ref[...] = jnp.zeros_like(c_ref)
        n_ref[...] = jnp.zeros_like(n_ref)
        m_ref[...] = jnp.zeros_like(m_ref)

    gc = _soft_cap(gcol_ref[...] + brow_ref[...])
    gr = _soft_cap(grow_ref[...] + bcol_ref[...])
    lsc = _log_sigmoid(gc)
    lsr = _log_sigmoid(gr)
    row = lax.broadcasted_iota(I32, (L, L), 0)
    col = lax.broadcasted_iota(I32, (L, L), 1)
    lower = col <= row
    upper = col >= row
    mask_t, mask_s = (upper, lower) if reverse else (lower, upper)
    scale = dk ** -0.5

    for h in range(heads):
        ci = (2 * heads if reverse else 0) + h
        cf = ci + heads
        i_col = gc[:, ci:ci + 1]
        lf_col = lsc[:, cf:cf + 1]
        i_row = gr[ci:ci + 1, :]
        lf_row = lsr[cf:cf + 1, :]
        b_col = jnp.sum(jnp.where(mask_t, lf_row, 0.0), axis=1, keepdims=True)
        b_row = jnp.sum(jnp.where(mask_s, lf_col, 0.0), axis=0, keepdims=True)
        g = jnp.sum(lf_row, axis=1, keepdims=True)
        m_prev = m_ref[h]
        a_col = g - b_col + i_col
        a_row = g - b_row + i_row
        m_new = jnp.maximum(g + m_prev, jnp.max(a_row, axis=1, keepdims=True))
        decay = jnp.exp(g + m_prev - m_new)

        q = q_ref[:, h * dk:(h + 1) * dk]
        k = k_ref[:, h * dk:(h + 1) * dk]
        v = v_ref[:, h * dv:(h + 1) * dv]
        qs = q.astype(F32) * scale
        qs_b = qs.astype(BF16)
        c_old = c_ref[h]
        n_old = n_ref[h]
        num_i = jnp.dot(qs_b, c_old.astype(BF16), preferred_element_type=F32)
        den_i = jnp.sum(qs * n_old, axis=1, keepdims=True)

        log_inter = b_col + m_prev
        dlog = jnp.where(mask_t, b_col - b_row + i_row, -jnp.inf)
        m_t = jnp.maximum(log_inter, jnp.max(dlog, axis=1, keepdims=True))
        qk = lax.dot_general(qs_b, k, (((1,), (1,)), ((), ())), preferred_element_type=F32)
        s = qk * jnp.exp(dlog - m_t)
        inter = jnp.exp(log_inter - m_t)
        num = inter * num_i + jnp.dot(s.astype(BF16), v, preferred_element_type=F32)
        den = inter * den_i + jnp.sum(s, axis=1, keepdims=True)
        hh = num * (1.0 / jnp.maximum(jnp.abs(den), jnp.exp(-m_t)))

        kw = k.astype(F32) * jnp.exp(a_col - m_new)
        c_ref[h] = decay * c_old + lax.dot_general(
            kw.astype(BF16), v, (((0,), (0,)), ((), ())), preferred_element_type=F32)
        n_ref[h] = decay * n_old + jnp.sum(kw, axis=0, keepdims=True)
        m_ref[h] = m_new

        hs = slice(h * dv, (h + 1) * dv)
        if fuse:
            tot = hf_ref[:, hs] + hh
            ms = jnp.mean(tot * tot, axis=-1, keepdims=True)
            gated = tot * lax.rsqrt(ms + EPS) * gh_ref[:, hs] * _sigmoid(o_ref[:, hs].astype(F32))
            out_ref[:, hs] = gated.astype(out_ref.dtype)
        else:
            out_ref[:, hs] = hh


def _mlstm(z, gcol, grow, brow, bcol, *, reverse, fused=None):
    b, s, _ = z.shape
    heads = MLSTM_HEADS
    vdim = fused[1].shape[-1] if fused is not None else None
    qk = (z.shape[-1]) // 6 if vdim is None else (z.shape[-1] - 2 * vdim) // 2
    vdim = 2 * qk if vdim is None else vdim
    dk, dv = qk // heads, vdim // heads
    nc = s // MLSTM_CHUNK
    L = MLSTM_CHUNK
    ng = grow.shape[1]

    def ch(c):
        return nc - 1 - c if reverse else c

    in_specs = [
        pl.BlockSpec((None, L, qk), lambda bi, c: (bi, ch(c), 0)),
        pl.BlockSpec((None, L, qk), lambda bi, c: (bi, ch(c), 1)),
        pl.BlockSpec((None, L, vdim), lambda bi, c: (bi, ch(c), (2 * qk) // vdim)),
        pl.BlockSpec((None, L, N_GATE_LANES), lambda bi, c: (bi, ch(c), 0)),
        pl.BlockSpec((None, ng, L), lambda bi, c: (bi, 0, ch(c))),
        pl.BlockSpec((1, N_GATE_LANES), lambda bi, c: (0, 0)),
        pl.BlockSpec((ng, L), lambda bi, c: (0, 0)),
    ]
    args = [z, z, z, gcol, grow, brow, bcol]
    if fused is not None:
        hf, gh = fused
        in_specs += [
            pl.BlockSpec((None, L, vdim), lambda bi, c: (bi, ch(c), (2 * qk + vdim) // vdim)),
            pl.BlockSpec((None, L, vdim), lambda bi, c: (bi, ch(c), 0)),
            pl.BlockSpec((1, vdim), lambda bi, c: (0, 0)),
        ]
        args += [z, hf, gh]
    return pl.pallas_call(
        functools.partial(_mlstm_kernel, reverse=reverse, fuse=fused is not None,
                          heads=heads, dk=dk, dv=dv),
        grid=(b, nc),
        in_specs=in_specs,
        out_specs=pl.BlockSpec((None, L, vdim), lambda bi, c: (bi, ch(c), 0)),
        out_shape=jax.ShapeDtypeStruct((b, s, vdim), BF16 if fused is not None else F32),
        scratch_shapes=[
            pltpu.VMEM((heads, dk, dv), F32),
            pltpu.VMEM((heads, 1, dk), F32),
            pltpu.VMEM((heads, 1, 1), F32),
        ],
        compiler_params=_cparams("parallel", "arbitrary"),
        name="mlstm_bwd" if reverse else "mlstm_fwd",
    )(*args)


def _outproj_kernel(h_ref, w_ref, x_ref, o_ref):
    o_ref[...] = x_ref[...] + jnp.dot(h_ref[...], w_ref[...], preferred_element_type=F32)


def _outproj(h, w, x):
    t, kdim = h.shape
    n = w.shape[1]
    tm = _pick(t, (1024, 512, 256, 128))
    tn = _pick(n, (512, 256, 128))
    return pl.pallas_call(
        _outproj_kernel,
        grid=(t // tm, n // tn),
        in_specs=[
            pl.BlockSpec((tm, kdim), lambda i, j: (i, 0)),
            pl.BlockSpec((kdim, tn), lambda i, j: (0, j)),
            pl.BlockSpec((tm, tn), lambda i, j: (i, j)),
        ],
        out_specs=pl.BlockSpec((tm, tn), lambda i, j: (i, j)),
        out_shape=jax.ShapeDtypeStruct((t, n), F32),
        compiler_params=_cparams("parallel", "arbitrary"),
        name="outproj",
    )(h, w, x)


def _inproj_sgu_kernel(x_ref, g_ref, w_ref, z_ref, ssq_ref, xn_ref, *, n_u_tiles):
    j = pl.program_id(1)

    @pl.when(j == 0)
    def _():
        _rmsnorm_rows(x_ref, g_ref, xn_ref)

    z = jax.nn.gelu(jnp.dot(xn_ref[...], w_ref[...], preferred_element_type=F32))
    z_ref[...] = z.astype(z_ref.dtype)

    @pl.when(j == n_u_tiles)
    def _():
        ssq_ref[...] = jnp.zeros_like(ssq_ref)

    @pl.when(j >= n_u_tiles)
    def _():
        z2 = z * z
        part = z2[:, 0:LANES]
        for c in range(1, z2.shape[1] // LANES):
            part = part + z2[:, c * LANES:(c + 1) * LANES]
        ssq_ref[...] += part


def _inproj_sgu(x, g, w):
    t, d = x.shape
    n = w.shape[1]
    half = n // 2
    tm = _pick(t, (1024, 512, 256, 128))
    tn = _pick(half, (512, 256, 128))
    return pl.pallas_call(
        functools.partial(_inproj_sgu_kernel, n_u_tiles=half // tn),
        grid=(t // tm, n // tn),
        in_specs=[
            pl.BlockSpec((tm, d), lambda i, j: (i, 0)),
            pl.BlockSpec((1, d), lambda i, j: (0, 0)),
            pl.BlockSpec((d, tn), lambda i, j: (0, j)),
        ],
        out_specs=[
            pl.BlockSpec((tm, tn), lambda i, j: (i, j)),
            pl.BlockSpec((tm, LANES), lambda i, j: (i, 0)),
        ],
        out_shape=[
            jax.ShapeDtypeStruct((t, n), BF16),
            jax.ShapeDtypeStruct((t, LANES), F32),
        ],
        scratch_shapes=[pltpu.VMEM((tm, d), BF16)],
        compiler_params=_cparams("parallel", "arbitrary"),
        name="inproj_sgu",
    )(x, g, w)


def _sgu_mix_kernel(u_ref, v_ref, ssq_ref, gv_ref, ws_ref, bs_ref, wo_ref, x_ref, o_ref, gated_ref,
                    *, groups, gd):
    rows = u_ref.shape[0]
    width = v_ref.shape[1]
    gv = gv_ref[...]
    for c in range(rows // SGU_CHUNK):
        rs = slice(c * SGU_CHUNK, (c + 1) * SGU_CHUNK)
        ms = jnp.sum(ssq_ref[rs, :], axis=-1, keepdims=True) / width
        vn = (v_ref[rs, :].astype(F32) * lax.rsqrt(ms + EPS) * gv).astype(BF16)
        for gi in range(groups):
            cs = slice(gi * gd, (gi + 1) * gd)
            mixed = jnp.dot(ws_ref[gi], vn[:, cs], preferred_element_type=F32)
            bias = bs_ref[gi]
            bias = jnp.concatenate([bias] * (gd // LANES), axis=1) if gd > LANES else bias
            gated_ref[rs, cs] = (u_ref[rs, cs].astype(F32) * (mixed + bias)).astype(BF16)
    o_ref[...] = x_ref[...] + jnp.dot(gated_ref[...], wo_ref[...], preferred_element_type=F32)


def _sgu_mix(z, ssq, gv, ws, bs, wo, x):
    t, d = x.shape
    width = z.shape[1] // 2
    groups = ws.shape[0]
    tm = _pick(t, (512, 256, 128))
    return pl.pallas_call(
        functools.partial(_sgu_mix_kernel, groups=groups, gd=width // groups),
        grid=(t // tm,),
        in_specs=[
            pl.BlockSpec((tm, width), lambda i: (i, 0)),
            pl.BlockSpec((tm, width), lambda i: (i, 1)),
            pl.BlockSpec((tm, LANES), lambda i: (i, 0)),
            pl.BlockSpec((1, width), lambda i: (0, 0)),
            pl.BlockSpec((groups, SGU_CHUNK, SGU_CHUNK), lambda i: (0, 0, 0)),
            pl.BlockSpec((groups, SGU_CHUNK, LANES), lambda i: (0, 0, 0)),
            pl.BlockSpec((width, d), lambda i: (0, 0)),
            pl.BlockSpec((tm, d), lambda i: (i, 0)),
        ],
        out_specs=pl.BlockSpec((tm, d), lambda i: (i, 0)),
        out_shape=jax.ShapeDtypeStruct((t, d), F32),
        scratch_shapes=[pltpu.VMEM((tm, width), BF16)],
        compiler_params=_cparams("parallel"),
        name="sgu_mix",
    )(z, z, ssq, gv, ws, bs, wo, x)


def _router_kernel(x_ref, g_ref, w_ref, b_ref, meta_ref, cnt_ref, hn_ref, carry_ref, *, n_grp, epg):
    tm = x_ref.shape[0]

    @pl.when(pl.program_id(0) == 0)
    def _():
        carry_ref[...] = jnp.zeros_like(carry_ref)

    _rmsnorm_rows(x_ref, g_ref, hn_ref)
    logits = jnp.dot(hn_ref[...], w_ref[...], preferred_element_type=F32,
                     precision=lax.Precision.HIGHEST) + b_ref[...]
    lane = lax.broadcasted_iota(I32, (tm, LANES), 1).astype(F32)
    big = float(LANES)

    gl = jnp.where(lane < n_grp, logits, -jnp.inf)
    gmax = jnp.max(gl, axis=1, keepdims=True)
    grp = jnp.min(jnp.where(gl == gmax, lane, big), axis=1, keepdims=True)
    grp_p = 1.0 / jnp.sum(jnp.exp(gl - gmax), axis=1, keepdims=True)

    lo = n_grp + grp * epg
    el = jnp.where(lane >= lo, jnp.where(lane < lo + epg, logits, -jnp.inf), -jnp.inf)
    emax = jnp.max(el, axis=1, keepdims=True)
    ee = jnp.exp(el - emax)
    p = ee / jnp.sum(ee, axis=1, keepdims=True)
    pm = jnp.where(lane >= lo, jnp.where(lane < lo + epg, p, -1.0), -1.0)
    p1 = jnp.max(pm, axis=1, keepdims=True)
    i1 = jnp.min(jnp.where(pm == p1, lane, big), axis=1, keepdims=True)
    pm2 = jnp.where(lane == i1, -1.0, pm)
    p2 = jnp.max(pm2, axis=1, keepdims=True)
    i2 = jnp.min(jnp.where(pm2 == p2, lane, big), axis=1, keepdims=True)
    psum = p1 + p2
    g1 = grp_p * (p1 / psum)
    g2 = grp_p * (p2 / psum)

    hit1 = lane == i1
    hit2 = lane == i2
    onehot = jnp.where(hit1, 1.0, jnp.where(hit2, 1.0, 0.0))
    r = lax.broadcasted_iota(I32, (tm, tm), 0)
    c = lax.broadcasted_iota(I32, (tm, tm), 1)
    tri = jnp.where(c < r, 1.0, 0.0).astype(BF16)
    carry = carry_ref[...]
    before = jnp.dot(tri, onehot.astype(BF16), preferred_element_type=F32) + carry
    rank1 = jnp.sum(jnp.where(hit1, before, 0.0), axis=1, keepdims=True)
    rank2 = jnp.sum(jnp.where(hit2, before, 0.0), axis=1, keepdims=True)
    carry = carry + jnp.sum(onehot, axis=0, keepdims=True)
    carry_ref[...] = carry
    cnt_ref[...] = jnp.broadcast_to(carry, cnt_ref.shape)

    meta = jnp.where(lane == 0, i1 - n_grp, 0.0)
    meta = jnp.where(lane == 1, i2 - n_grp, meta)
    meta = jnp.where(lane == 2, rank1, meta)
    meta = jnp.where(lane == 3, rank2, meta)
    meta = jnp.where(lane == 4, g1, meta)
    meta = jnp.where(lane == 5, g2, meta)
    meta_ref[...] = meta


def _router(x, g, w, b, *, n_grp, epg):
    t, d = x.shape
    tm = _pick(t, (512, 256, 128))
    return pl.pallas_call(
        functools.partial(_router_kernel, n_grp=n_grp, epg=epg),
        grid=(t // tm,),
        in_specs=[
            pl.BlockSpec((tm, d), lambda i: (i, 0)),
            pl.BlockSpec((1, d), lambda i: (0, 0)),
            pl.BlockSpec((d, LANES), lambda i: (0, 0)),
            pl.BlockSpec((1, LANES), lambda i: (0, 0)),
        ],
        out_specs=[
            pl.BlockSpec((tm, LANES), lambda i: (i, 0)),
            pl.BlockSpec((8, LANES), lambda i: (0, 0)),
        ],
        out_shape=[
            jax.ShapeDtypeStruct((t, LANES), F32),
            jax.ShapeDtypeStruct((8, LANES), F32),
        ],
        scratch_shapes=[pltpu.VMEM((tm, d), F32), pltpu.VMEM((1, LANES), F32)],
        compiler_params=_cparams("arbitrary"),
        name="router",
    )(x, g, w, b)


def _dispatch_kernel(d0_ref, d1_ref, x_ref, g_ref, xb_in_ref, xb_ref, pk_ref, sem):
    del xb_in_ref
    tm, d = x_ref.shape
    half = d // 2
    g = g_ref[...]

    def pack(i, c):
        r = pl.multiple_of(i * ROW_CHUNK, ROW_CHUNK)
        x = x_ref[pl.ds(r, ROW_CHUNK), :]
        ms = jnp.mean(x * x, axis=-1, keepdims=True)
        hn = (x * lax.rsqrt(ms + EPS) * g).astype(BF16).astype(F32)
        bits = lax.bitcast_convert_type(hn, U32)
        lo = lax.shift_right_logical(bits[:, :half], jnp.uint32(16))
        hi = bits[:, half:] & jnp.uint32(0xFFFF0000)
        pk_ref[pl.ds(r, ROW_CHUNK), :] = hi | lo
        return c

    lax.fori_loop(0, tm // ROW_CHUNK, pack, 0)

    def issue(r, c):
        src = pk_ref.at[pl.ds(r, 1)]
        pltpu.make_async_copy(src, xb_ref.at[pl.ds(d0_ref[r], 1)], sem.at[0]).start()
        pltpu.make_async_copy(src, xb_ref.at[pl.ds(d1_ref[r], 1)], sem.at[0]).start()
        return c

    lax.fori_loop(0, tm, issue, 0)
    for _ in range(TOP_K):
        pltpu.make_async_copy(pk_ref, xb_ref.at[pl.ds(0, tm)], sem.at[0]).wait()


def _dispatch(d0, d1, x, g, xb_init):
    t, d = x.shape
    tm = _pick(t, (256, 128))
    return pl.pallas_call(
        _dispatch_kernel,
        grid=(t // tm,),
        in_specs=[
            pl.BlockSpec((tm,), lambda i: (i,), memory_space=pltpu.SMEM),
            pl.BlockSpec((tm,), lambda i: (i,), memory_space=pltpu.SMEM),
            pl.BlockSpec((tm, d), lambda i: (i, 0)),
            pl.BlockSpec((1, d), lambda i: (0, 0)),
            pl.BlockSpec(memory_space=pl.ANY),
        ],
        out_specs=pl.BlockSpec(memory_space=pl.ANY),
        out_shape=jax.ShapeDtypeStruct(xb_init.shape, xb_init.dtype),
        scratch_shapes=[pltpu.VMEM((tm, d // 2), U32), pltpu.SemaphoreType.DMA((1,))],
        input_output_aliases={4: 0},
        compiler_params=_cparams("arbitrary"),
        name="moe_dispatch",
    )(d0, d1, x, g, xb_init)


def _experts_kernel(be_ref, nu_ref, xb_ref, wg_ref, wu_ref, wd_ref, yb_ref):
    del be_ref
    half = xb_ref.shape[1]

    @pl.when(pl.program_id(0) < nu_ref[0])
    def _():
        w = xb_ref[...]
        x_lo = lax.bitcast_convert_type(lax.shift_left(w, jnp.uint32(16)), F32).astype(BF16)
        x_hi = lax.bitcast_convert_type(w & jnp.uint32(0xFFFF0000), F32).astype(BF16)
        h1 = (jnp.dot(x_lo, wg_ref[:half, :], preferred_element_type=F32)
              + jnp.dot(x_hi, wg_ref[half:, :], preferred_element_type=F32))
        h2 = (jnp.dot(x_lo, wu_ref[:half, :], preferred_element_type=F32)
              + jnp.dot(x_hi, wu_ref[half:, :], preferred_element_type=F32))
        hid = (h1 * _sigmoid(h1)) * h2
        yb_ref[...] = jnp.dot(hid.astype(BF16), wd_ref[...], preferred_element_type=F32)

    @pl.when(pl.program_id(0) >= nu_ref[0])
    def _():
        yb_ref[...] = jnp.zeros_like(yb_ref)


def _experts(blk_exp, n_used, xb, wg, wu, wd):
    p, half = xb.shape
    d = 2 * half
    de = wg.shape[-1]
    n_blk = p // MOE_BLOCK

    def live(i, nu):
        return jnp.minimum(i, nu[0] - 1)

    return pl.pallas_call(
        _experts_kernel,
        grid_spec=pltpu.PrefetchScalarGridSpec(
            num_scalar_prefetch=2,
            grid=(n_blk,),
            in_specs=[
                pl.BlockSpec((MOE_BLOCK, half), lambda i, be, nu: (live(i, nu), 0)),
                pl.BlockSpec((None, d, de), lambda i, be, nu: (be[live(i, nu)], 0, 0)),
                pl.BlockSpec((None, d, de), lambda i, be, nu: (be[live(i, nu)], 0, 0)),
                pl.BlockSpec((None, de, d), lambda i, be, nu: (be[live(i, nu)], 0, 0)),
            ],
            out_specs=pl.BlockSpec((MOE_BLOCK, d), lambda i, be, nu: (i, 0)),
        ),
        out_shape=jax.ShapeDtypeStruct((p, d), F32),
        compiler_params=_cparams("arbitrary"),
        name="moe_experts",
    )(blk_exp, n_used, xb, wg, wu, wd)


def _combine_kernel(*refs, final):
    if final:
        d0_ref, d1_ref, x_ref, meta_ref, yb_ref, gf_ref, o_ref, a_ref, b_ref, sem = refs
    else:
        d0_ref, d1_ref, x_ref, meta_ref, yb_ref, o_ref, a_ref, b_ref, sem = refs
    tm = x_ref.shape[0]

    def issue(r, c):
        pltpu.make_async_copy(yb_ref.at[pl.ds(d0_ref[r], 1)], a_ref.at[pl.ds(r, 1)], sem.at[0]).start()
        pltpu.make_async_copy(yb_ref.at[pl.ds(d1_ref[r], 1)], b_ref.at[pl.ds(r, 1)], sem.at[1]).start()
        return c

    lax.fori_loop(0, tm, issue, 0)
    pltpu.make_async_copy(yb_ref.at[pl.ds(0, tm)], a_ref, sem.at[0]).wait()
    pltpu.make_async_copy(yb_ref.at[pl.ds(0, tm)], b_ref, sem.at[1]).wait()

    def body(i, c):
        r = pl.multiple_of(i * ROW_CHUNK, ROW_CHUNK)
        rs = pl.ds(r, ROW_CHUNK)
        meta = meta_ref[rs, :]
        y = meta[:, 4:5] * a_ref[rs, :] + meta[:, 5:6] * b_ref[rs, :]
        xn = x_ref[rs, :] + y
        if final:
            ms = jnp.mean(xn * xn, axis=-1, keepdims=True)
            xn = xn * lax.rsqrt(ms + EPS) * gf_ref[...]
        o_ref[rs, :] = xn
        return c

    lax.fori_loop(0, tm // ROW_CHUNK, body, 0)


def _combine(d0, d1, x, meta, yb, g_final):
    t, d = x.shape
    tm = _pick(t, (256, 128))
    final = g_final is not None
    in_specs = [
        pl.BlockSpec((tm,), lambda i: (i,), memory_space=pltpu.SMEM),
        pl.BlockSpec((tm,), lambda i: (i,), memory_space=pltpu.SMEM),
        pl.BlockSpec((tm, d), lambda i: (i, 0)),
        pl.BlockSpec((tm, LANES), lambda i: (i, 0)),
        pl.BlockSpec(memory_space=pl.ANY),
    ]
    args = [d0, d1, x, meta, yb]
    if final:
        in_specs.append(pl.BlockSpec((1, d), lambda i: (0, 0)))
        args.append(g_final)
    return pl.pallas_call(
        functools.partial(_combine_kernel, final=final),
        grid=(t // tm,),
        in_specs=in_specs,
        out_specs=pl.BlockSpec((tm, d), lambda i: (i, 0)),
        out_shape=jax.ShapeDtypeStruct((t, d), F32),
        scratch_shapes=[pltpu.VMEM((tm, d), F32), pltpu.VMEM((tm, d), F32),
                        pltpu.SemaphoreType.DMA((2,))],
        compiler_params=_cparams("arbitrary"),
        name="moe_combine",
    )(*args)


def _moe(x, g_norm, w_route, b_route, wg, wu, wd, g_final, *, n_grp, epg):
    t, d = x.shape
    n_exp = n_grp * epg
    meta, cnt = _router(x, g_norm, w_route, b_route, n_grp=n_grp, epg=epg)

    counts = cnt[0, n_grp:n_grp + n_exp].astype(I32)
    pcounts = (counts + MOE_BLOCK - 1) // MOE_BLOCK * MOE_BLOCK
    pends = jnp.cumsum(pcounts)
    pstarts = pends - pcounts
    n_blk = (t * TOP_K) // MOE_BLOCK + n_exp
    e0 = meta[:, 0].astype(I32)
    e1 = meta[:, 1].astype(I32)
    d0 = pstarts[e0] + meta[:, 2].astype(I32)
    d1 = pstarts[e1] + meta[:, 3].astype(I32)
    blk_exp = jnp.minimum(
        jnp.searchsorted(pends, jnp.arange(n_blk, dtype=I32) * MOE_BLOCK, side="right"),
        n_exp - 1).astype(I32)
    n_used = (pends[-1:] // MOE_BLOCK).astype(I32)

    xb = _dispatch(d0, d1, x, g_norm, jnp.zeros((n_blk * MOE_BLOCK, d // 2), U32))
    yb = _experts(blk_exp, n_used, xb, wg, wu, wd)
    return _combine(d0, d1, x, meta, yb, g_final)


def _row(v):
    return v.reshape(1, -1).astype(F32)


def _trunk(x, p):
    b, s, d = x.shape
    t = b * s
    depth = p["norm_mix"].shape[0]
    xt = x.reshape(t, d)
    for i in range(depth):
        j = i // 2
        if i % 2 == 0:
            z, gates = _inproj_mlstm(xt, _row(p["norm_mix"][i]), p["mlstm_w_main"][j], p["mlstm_w_gate"][j])
            ng = 4 * MLSTM_HEADS
            gcol = gates.reshape(b, s, N_GATE_LANES)
            grow = jnp.swapaxes(gcol[:, :, :ng], 1, 2)
            brow, bcol = p["mlstm_b_row"][j], p["mlstm_b_col"][j]
            z3 = z.reshape(b, s, -1)
            hf = _mlstm(z3, gcol, grow, brow, bcol, reverse=False)
            hn = _mlstm(z3, gcol, grow, brow, bcol, reverse=True, fused=(hf, _row(p["mlstm_g_hnorm"][j])))
            xt = _outproj(hn.reshape(t, -1), p["mlstm_w_out"][j], xt)
        else:
            z, ssq = _inproj_sgu(xt, _row(p["norm_mix"][i]), p["sgu_w_in"][j])
            xt = _sgu_mix(z, ssq, _row(p["sgu_g_v"][j]), p["sgu_w_s"][j], p["sgu_b_s"][j],
                          p["sgu_w_out"][j], xt)
        g_final = _row(p["norm_final"]) if i == depth - 1 else None
        xt = _moe(xt, _row(p["norm_ffn"][i]), p["moe_w_route"][i], p["moe_b_route"][i],
                  p["moe_w_gate"][i], p["moe_w_up"][i], p["moe_w_down"][i], g_final,
                  n_grp=p["n_grp"], epg=p["epg"])
    if depth == 0:
        raise ValueError("depth 0 is not supported")
    return xt.reshape(b, s, d)


def kernel(x_prompt, x_sample, norm_mix, norm_ffn, norm_final, mlstm_w_in, mlstm_b_gates, mlstm_g_hnorm, mlstm_w_out, sgu_w_in, sgu_g_v, sgu_w_s, sgu_b_s, sgu_w_out, moe_w_grp, moe_b_grp, moe_w_exp, moe_b_exp, moe_w_gate, moe_w_up, moe_w_down):
    ng = 4 * MLSTM_HEADS
    n_main = mlstm_w_in.shape[-1] - ng
    n_grp = moe_w_grp.shape[-1]
    n_exp = moe_w_exp.shape[-1]
    assert n_grp + n_exp <= LANES and ng <= N_GATE_LANES

    def pad_lanes(a, width):
        return jnp.pad(a, [(0, 0)] * (a.ndim - 1) + [(0, width - a.shape[-1])])

    p = dict(
        norm_mix=norm_mix, norm_ffn=norm_ffn, norm_final=norm_final,
        mlstm_w_main=mlstm_w_in[..., :n_main].astype(BF16),
        mlstm_w_gate=pad_lanes(mlstm_w_in[..., n_main:], N_GATE_LANES).astype(BF16),
        mlstm_b_row=pad_lanes(mlstm_b_gates, N_GATE_LANES)[:, None, :].astype(F32),
        mlstm_b_col=jnp.broadcast_to(mlstm_b_gates[:, :, None], mlstm_b_gates.shape + (MLSTM_CHUNK,)).astype(F32),
        mlstm_g_hnorm=mlstm_g_hnorm,
        mlstm_w_out=mlstm_w_out.astype(BF16),
        sgu_w_in=sgu_w_in.astype(BF16),
        sgu_g_v=sgu_g_v,
        sgu_w_s=sgu_w_s.astype(BF16),
        sgu_b_s=jnp.broadcast_to(sgu_b_s[..., None], sgu_b_s.shape + (LANES,)).astype(F32),
        sgu_w_out=sgu_w_out.astype(BF16),
        moe_w_route=pad_lanes(jnp.concatenate([moe_w_grp, moe_w_exp], axis=-1), LANES).astype(F32),
        moe_b_route=pad_lanes(jnp.concatenate([moe_b_grp, moe_b_exp], axis=-1), LANES)[:, None, :].astype(F32),
        moe_w_gate=moe_w_gate.astype(BF16),
        moe_w_up=moe_w_up.astype(BF16),
        moe_w_down=moe_w_down.astype(BF16),
        n_grp=n_grp, epg=n_exp // n_grp,
    )
    return (_trunk(x_prompt, p), _trunk(x_sample, p))
```

```python
import functools

import jax
import jax.numpy as jnp
from jax import lax
from jax.experimental import pallas as pl
from jax.experimental.pallas import tpu as pltpu

F32 = jnp.float32
BF16 = jnp.bfloat16
U32 = jnp.uint32
I32 = jnp.int32

EPS = 1e-6
MLSTM_HEADS = 4
MLSTM_CHUNK = 128
GATE_SOFTCAP = 15.0
N_GATE_LANES = 128
SGU_CHUNK = 128
TOP_K = 2
ROW_CHUNK = 128
MOE_BLOCK = 256
DMA_UNROLL = 8
LANES = 128
VMEM_LIMIT_BYTES = 56 * 1024 * 1024


def _cparams(*sem):
    return pltpu.CompilerParams(dimension_semantics=sem, vmem_limit_bytes=VMEM_LIMIT_BYTES)


def _pick(n, prefs):
    for p in prefs:
        if n % p == 0:
            return p
    return n


def _rmsnorm_rows(x_ref, g_ref, out_ref):
    rows = x_ref.shape[0]
    g = g_ref[...]

    def body(i, c):
        r = pl.multiple_of(i * ROW_CHUNK, ROW_CHUNK)
        x = x_ref[pl.ds(r, ROW_CHUNK), :]
        ms = jnp.mean(x * x, axis=-1, keepdims=True)
        out_ref[pl.ds(r, ROW_CHUNK), :] = (x * lax.rsqrt(ms + EPS) * g).astype(out_ref.dtype)
        return c

    lax.fori_loop(0, rows // ROW_CHUNK, body, 0)


def _sigmoid(x):
    return 1.0 / (1.0 + jnp.exp(-x))


def _log_sigmoid(x):
    return jnp.minimum(x, 0.0) - jnp.log1p(jnp.exp(-jnp.abs(x)))


def _soft_cap(t):
    return GATE_SOFTCAP * jnp.tanh(t / GATE_SOFTCAP)


def _inproj_mlstm_kernel(x_ref, g_ref, w_ref, wg_ref, z_ref, gate_ref, xn_ref):
    @pl.when(pl.program_id(1) == 0)
    def _():
        _rmsnorm_rows(x_ref, g_ref, xn_ref)
        gate_ref[...] = jnp.dot(xn_ref[...], wg_ref[...], preferred_element_type=F32)

    z_ref[...] = jnp.dot(xn_ref[...], w_ref[...], preferred_element_type=F32).astype(z_ref.dtype)


def _inproj_mlstm(x, g, w, wg):
    t, d = x.shape
    n = w.shape[1]
    tm = _pick(t, (1024, 512, 256, 128))
    tn = _pick(n, (512, 256, 128))
    return pl.pallas_call(
        _inproj_mlstm_kernel,
        grid=(t // tm, n // tn),
        in_specs=[
            pl.BlockSpec((tm, d), lambda i, j: (i, 0)),
            pl.BlockSpec((1, d), lambda i, j: (0, 0)),
            pl.BlockSpec((d, tn), lambda i, j: (0, j)),
            pl.BlockSpec((d, N_GATE_LANES), lambda i, j: (0, 0)),
        ],
        out_specs=[
            pl.BlockSpec((tm, tn), lambda i, j: (i, j)),
            pl.BlockSpec((tm, N_GATE_LANES), lambda i, j: (i, 0)),
        ],
        out_shape=[
            jax.ShapeDtypeStruct((t, n), BF16),
            jax.ShapeDtypeStruct((t, N_GATE_LANES), F32),
        ],
        scratch_shapes=[pltpu.VMEM((tm, d), BF16)],
        compiler_params=_cparams("parallel", "arbitrary"),
        name="inproj_mlstm",
    )(x, g, w, wg)


def _mlstm_kernel(*refs, reverse, fuse, heads, dk, dv):
    if fuse:
        (q_ref, k_ref, v_ref, gcol_ref, grow_ref, brow_ref, bcol_ref, o_ref, hf_ref, gh_ref,
         out_ref, c_ref, n_ref, m_ref) = refs
    else:
        (q_ref, k_ref, v_ref, gcol_ref, grow_ref, brow_ref, bcol_ref,
         out_ref, c_ref, n_ref, m_ref) = refs
    L = MLSTM_CHUNK

    @pl.when(pl.program_id(1) == 0)
    def _():
        c_ref[...] = jnp.zeros_like(c_ref)
        n_ref[...] = jnp.zeros_like(n_ref)
        m_ref[...] = jnp.zeros_like(m_ref)

    gc = _soft_cap(gcol_ref[...] + brow_ref[...])
    gr = _soft_cap(grow_ref[...] + bcol_ref[...])
    lsc = _log_sigmoid(gc)
    lsr = _log_sigmoid(gr)
    row = lax.broadcasted_iota(I32, (L, L), 0)
    col = lax.broadcasted_iota(I32, (L, L), 1)
    lower = col <= row
    upper = col >= row
    mask_t, mask_s = (upper, lower) if reverse else (lower, upper)
    scale = dk ** -0.5

    for h in range(heads):
        ci = (2 * heads if reverse else 0) + h
        cf = ci + heads
        i_col = gc[:, ci:ci + 1]
        lf_col = lsc[:, cf:cf + 1]
        i_row = gr[ci:ci + 1, :]
        lf_row = lsr[cf:cf + 1, :]
        b_col = jnp.sum(jnp.where(mask_t, lf_row, 0.0), axis=1, keepdims=True)
        b_row = jnp.sum(jnp.where(mask_s, lf_col, 0.0), axis=0, keepdims=True)
        g = jnp.sum(lf_row, axis=1, keepdims=True)
        m_prev = m_ref[h]
        a_col = g - b_col + i_col
        a_row = g - b_row + i_row
        m_new = jnp.maximum(g + m_prev, jnp.max(a_row, axis=1, keepdims=True))
        decay = jnp.exp(g + m_prev - m_new)

        q = q_ref[:, h * dk:(h + 1) * dk]
        k = k_ref[:, h * dk:(h + 1) * dk]
        v = v_ref[:, h * dv:(h + 1) * dv]
        qs = q.astype(F32) * scale
        qs_b = qs.astype(BF16)
        c_old = c_ref[h]
        n_old = n_ref[h]
        num_i = jnp.dot(qs_b, c_old.astype(BF16), preferred_element_type=F32)
        den_i = jnp.sum(qs * n_old, axis=1, keepdims=True)

        log_inter = b_col + m_prev
        dlog = jnp.where(mask_t, b_col - b_row + i_row, -jnp.inf)
        m_t = jnp.maximum(log_inter, jnp.max(dlog, axis=1, keepdims=True))
        qk = lax.dot_general(qs_b, k, (((1,), (1,)), ((), ())), preferred_element_type=F32)
        s = qk * jnp.exp(dlog - m_t)
        inter = jnp.exp(log_inter - m_t)
        num = inter * num_i + jnp.dot(s.astype(BF16), v, preferred_element_type=F32)
        den = inter * den_i + jnp.sum(s, axis=1, keepdims=True)
        hh = num * (1.0 / jnp.maximum(jnp.abs(den), jnp.exp(-m_t)))

        kw = k.astype(F32) * jnp.exp(a_col - m_new)
        c_ref[h] = decay * c_old + lax.dot_general(
            kw.astype(BF16), v, (((0,), (0,)), ((), ())), preferred_element_type=F32)
        n_ref[h] = decay * n_old + jnp.sum(kw, axis=0, keepdims=True)
        m_ref[h] = m_new

        hs = slice(h * dv, (h + 1) * dv)
        if fuse:
            tot = hf_ref[:, hs] + hh
            ms = jnp.mean(tot * tot, axis=-1, keepdims=True)
            gated = tot * lax.rsqrt(ms + EPS) * gh_ref[:, hs] * _sigmoid(o_ref[:, hs].astype(F32))
            out_ref[:, hs] = gated.astype(out_ref.dtype)
        else:
            out_ref[:, hs] = hh


def _mlstm(z, gcol, grow, brow, bcol, *, reverse, fused=None):
    b, s, _ = z.shape
    heads = MLSTM_HEADS
    vdim = fused[1].shape[-1] if fused is not None else None
    qk = (z.shape[-1]) // 6 if vdim is None else (z.shape[-1] - 2 * vdim) // 2
    vdim = 2 * qk if vdim is None else vdim
    dk, dv = qk // heads, vdim // heads
    nc = s // MLSTM_CHUNK
    L = MLSTM_CHUNK
    ng = grow.shape[1]

    def ch(c):
        return nc - 1 - c if reverse else c

    in_specs = [
        pl.BlockSpec((None, L, qk), lambda bi, c: (bi, ch(c), 0)),
        pl.BlockSpec((None, L, qk), lambda bi, c: (bi, ch(c), 1)),
        pl.BlockSpec((None, L, vdim), lambda bi, c: (bi, ch(c), (2 * qk) // vdim)),
        pl.BlockSpec((None, L, N_GATE_LANES), lambda bi, c: (bi, ch(c), 0)),
        pl.BlockSpec((None, ng, L), lambda bi, c: (bi, 0, ch(c))),
        pl.BlockSpec((1, N_GATE_LANES), lambda bi, c: (0, 0)),
        pl.BlockSpec((ng, L), lambda bi, c: (0, 0)),
    ]
    args = [z, z, z, gcol, grow, brow, bcol]
    if fused is not None:
        hf, gh = fused
        in_specs += [
            pl.BlockSpec((None, L, vdim), lambda bi, c: (bi, ch(c), (2 * qk + vdim) // vdim)),
            pl.BlockSpec((None, L, vdim), lambda bi, c: (bi, ch(c), 0)),
            pl.BlockSpec((1, vdim), lambda bi, c: (0, 0)),
        ]
        args += [z, hf, gh]
    return pl.pallas_call(
        functools.partial(_mlstm_kernel, reverse=reverse, fuse=fused is not None,
                          heads=heads, dk=dk, dv=dv),
        grid=(b, nc),
        in_specs=in_specs,
        out_specs=pl.BlockSpec((None, L, vdim), lambda bi, c: (bi, ch(c), 0)),
        out_shape=jax.ShapeDtypeStruct((b, s, vdim), BF16 if fused is not None else F32),
        scratch_shapes=[
            pltpu.VMEM((heads, dk, dv), F32),
            pltpu.VMEM((heads, 1, dk), F32),
            pltpu.VMEM((heads, 1, 1), F32),
        ],
        compiler_params=_cparams("parallel", "arbitrary"),
        name="mlstm_bwd" if reverse else "mlstm_fwd",
    )(*args)


def _outproj_kernel(h_ref, w_ref, x_ref, o_ref):
    o_ref[...] = x_ref[...] + jnp.dot(h_ref[...], w_ref[...], preferred_element_type=F32)


def _outproj(h, w, x):
    t, kdim = h.shape
    n = w.shape[1]
    tm = _pick(t, (1024, 512, 256, 128))
    tn = _pick(n, (512, 256, 128))
    return pl.pallas_call(
        _outproj_kernel,
        grid=(t // tm, n // tn),
        in_specs=[
            pl.BlockSpec((tm, kdim), lambda i, j: (i, 0)),
            pl.BlockSpec((kdim, tn), lambda i, j: (0, j)),
            pl.BlockSpec((tm, tn), lambda i, j: (i, j)),
        ],
        out_specs=pl.BlockSpec((tm, tn), lambda i, j: (i, j)),
        out_shape=jax.ShapeDtypeStruct((t, n), F32),
        compiler_params=_cparams("parallel", "arbitrary"),
        name="outproj",
    )(h, w, x)


def _inproj_sgu_kernel(x_ref, g_ref, w_ref, z_ref, ssq_ref, xn_ref, *, n_u_tiles):
    j = pl.program_id(1)

    @pl.when(j == 0)
    def _():
        _rmsnorm_rows(x_ref, g_ref, xn_ref)

    z = jax.nn.gelu(jnp.dot(xn_ref[...], w_ref[...], preferred_element_type=F32))
    z_ref[...] = z.astype(z_ref.dtype)

    @pl.when(j == n_u_tiles)
    def _():
        ssq_ref[...] = jnp.zeros_like(ssq_ref)

    @pl.when(j >= n_u_tiles)
    def _():
        z2 = z * z
        part = z2[:, 0:LANES]
        for c in range(1, z2.shape[1] // LANES):
            part = part + z2[:, c * LANES:(c + 1) * LANES]
        ssq_ref[...] += part


def _inproj_sgu(x, g, w):
    t, d = x.shape
    n = w.shape[1]
    half = n // 2
    tm = _pick(t, (1024, 512, 256, 128))
    tn = _pick(half, (512, 256, 128))
    return pl.pallas_call(
        functools.partial(_inproj_sgu_kernel, n_u_tiles=half // tn),
        grid=(t // tm, n // tn),
        in_specs=[
            pl.BlockSpec((tm, d), lambda i, j: (i, 0)),
            pl.BlockSpec((1, d), lambda i, j: (0, 0)),
            pl.BlockSpec((d, tn), lambda i, j: (0, j)),
        ],
        out_specs=[
            pl.BlockSpec((tm, tn), lambda i, j: (i, j)),
            pl.BlockSpec((tm, LANES), lambda i, j: (i, 0)),
        ],
        out_shape=[
            jax.ShapeDtypeStruct((t, n), BF16),
            jax.ShapeDtypeStruct((t, LANES), F32),
        ],
        scratch_shapes=[pltpu.VMEM((tm, d), BF16)],
        compiler_params=_cparams("parallel", "arbitrary"),
        name="inproj_sgu",
    )(x, g, w)


def _sgu_mix_kernel(u_ref, v_ref, ssq_ref, gv_ref, ws_ref, bs_ref, wo_ref, x_ref, o_ref, gated_ref,
                    *, groups, gd):
    rows = u_ref.shape[0]
    width = v_ref.shape[1]
    gv = gv_ref[...]
    for c in range(rows // SGU_CHUNK):
        rs = slice(c * SGU_CHUNK, (c + 1) * SGU_CHUNK)
        ms = jnp.sum(ssq_ref[rs, :], axis=-1, keepdims=True) / width
        vn = (v_ref[rs, :].astype(F32) * lax.rsqrt(ms + EPS) * gv).astype(BF16)
        for gi in range(groups):
            cs = slice(gi * gd, (gi + 1) * gd)
            mixed = jnp.dot(ws_ref[gi], vn[:, cs], preferred_element_type=F32)
            bias = bs_ref[gi]
            bias = jnp.concatenate([bias] * (gd // LANES), axis=1) if gd > LANES else bias
            gated_ref[rs, cs] = (u_ref[rs, cs].astype(F32) * (mixed + bias)).astype(BF16)
    o_ref[...] = x_ref[...] + jnp.dot(gated_ref[...], wo_ref[...], preferred_element_type=F32)


def _sgu_mix(z, ssq, gv, ws, bs, wo, x):
    t, d = x.shape
    width = z.shape[1] // 2
    groups = ws.shape[0]
    tm = _pick(t, (512, 256, 128))
    return pl.pallas_call(
        functools.partial(_sgu_mix_kernel, groups=groups, gd=width // groups),
        grid=(t // tm,),
        in_specs=[
            pl.BlockSpec((tm, width), lambda i: (i, 0)),
            pl.BlockSpec((tm, width), lambda i: (i, 1)),
            pl.BlockSpec((tm, LANES), lambda i: (i, 0)),
            pl.BlockSpec((1, width), lambda i: (0, 0)),
            pl.BlockSpec((groups, SGU_CHUNK, SGU_CHUNK), lambda i: (0, 0, 0)),
            pl.BlockSpec((groups, SGU_CHUNK, LANES), lambda i: (0, 0, 0)),
            pl.BlockSpec((width, d), lambda i: (0, 0)),
            pl.BlockSpec((tm, d), lambda i: (i, 0)),
        ],
        out_specs=pl.BlockSpec((tm, d), lambda i: (i, 0)),
        out_shape=jax.ShapeDtypeStruct((t, d), F32),
        scratch_shapes=[pltpu.VMEM((tm, width), BF16)],
        compiler_params=_cparams("parallel"),
        name="sgu_mix",
    )(z, z, ssq, gv, ws, bs, wo, x)


def _router_kernel(x_ref, g_ref, w_ref, b_ref, eid_ref, rank_ref, gate_ref, cnt_ref, hn_ref, carry_ref,
                   *, n_grp, epg, rows):
    tm = x_ref.shape[0]

    @pl.when(pl.program_id(0) == 0)
    def _():
        carry_ref[...] = jnp.zeros_like(carry_ref)

    _rmsnorm_rows(x_ref, g_ref, hn_ref)
    logits = jnp.dot(hn_ref[...], w_ref[...], preferred_element_type=F32) + b_ref[...]
    lt = logits.T[:rows, :]
    ridx = lax.broadcasted_iota(I32, (rows, tm), 0).astype(F32)
    big = float(rows)

    gl = jnp.where(ridx < n_grp, lt, -jnp.inf)
    gmax = jnp.max(gl, axis=0, keepdims=True)
    grp = jnp.min(jnp.where(gl == gmax, ridx, big), axis=0, keepdims=True)
    grp_p = 1.0 / jnp.sum(jnp.exp(gl - gmax), axis=0, keepdims=True)

    lo = n_grp + grp * epg
    el = jnp.where(ridx >= lo, jnp.where(ridx < lo + epg, lt, -jnp.inf), -jnp.inf)
    emax = jnp.max(el, axis=0, keepdims=True)
    ee = jnp.exp(el - emax)
    p = ee / jnp.sum(ee, axis=0, keepdims=True)
    pm = jnp.where(ridx >= lo, jnp.where(ridx < lo + epg, p, -1.0), -1.0)
    p1 = jnp.max(pm, axis=0, keepdims=True)
    i1 = jnp.min(jnp.where(pm == p1, ridx, big), axis=0, keepdims=True)
    pm2 = jnp.where(ridx == i1, -1.0, pm)
    p2 = jnp.max(pm2, axis=0, keepdims=True)
    i2 = jnp.min(jnp.where(pm2 == p2, ridx, big), axis=0, keepdims=True)
    psum = p1 + p2
    g1 = grp_p * (p1 / psum)
    g2 = grp_p * (p2 / psum)

    hit1 = ridx == i1
    hit2 = ridx == i2
    onehot = jnp.where(hit1, 1.0, jnp.where(hit2, 1.0, 0.0))
    r = lax.broadcasted_iota(I32, (tm, tm), 0)
    c = lax.broadcasted_iota(I32, (tm, tm), 1)
    earlier = jnp.where(r < c, 1.0, 0.0).astype(BF16)
    carry = carry_ref[:, 0:1]
    before = jnp.dot(onehot.astype(BF16), earlier, preferred_element_type=F32) + carry
    rank1 = jnp.sum(jnp.where(hit1, before, 0.0), axis=0, keepdims=True)
    rank2 = jnp.sum(jnp.where(hit2, before, 0.0), axis=0, keepdims=True)
    carry = carry + jnp.sum(onehot, axis=1, keepdims=True)
    carry_ref[...] = jnp.broadcast_to(carry, carry_ref.shape)
    cnt_ref[...] = jnp.broadcast_to(carry, cnt_ref.shape)

    sub = lax.broadcasted_iota(I32, (8, tm), 0)
    eid_ref[...] = jnp.where(sub == 0, i1 - n_grp, jnp.where(sub == 1, i2 - n_grp, 0.0)).astype(I32)
    rank_ref[...] = jnp.where(sub == 0, rank1, jnp.where(sub == 1, rank2, 0.0)).astype(I32)
    sub_l = lax.broadcasted_iota(I32, (LANES, tm), 0)
    gate_ref[...] = jnp.where(sub_l == 0, g1, jnp.where(sub_l == 1, g2, 0.0)).T


def _router(x, g, w, b, *, n_grp, epg):
    t, d = x.shape
    tm = _pick(t, (512, 256, 128))
    rows = -(-(n_grp + n_grp * epg) // 8) * 8
    return pl.pallas_call(
        functools.partial(_router_kernel, n_grp=n_grp, epg=epg, rows=rows),
        grid=(t // tm,),
        in_specs=[
            pl.BlockSpec((tm, d), lambda i: (i, 0)),
            pl.BlockSpec((1, d), lambda i: (0, 0)),
            pl.BlockSpec((d, LANES), lambda i: (0, 0)),
            pl.BlockSpec((1, LANES), lambda i: (0, 0)),
        ],
        out_specs=[
            pl.BlockSpec((8, tm), lambda i: (0, i)),
            pl.BlockSpec((8, tm), lambda i: (0, i)),
            pl.BlockSpec((tm, LANES), lambda i: (i, 0)),
            pl.BlockSpec((rows, LANES), lambda i: (0, 0)),
        ],
        out_shape=[
            jax.ShapeDtypeStruct((8, t), I32),
            jax.ShapeDtypeStruct((8, t), I32),
            jax.ShapeDtypeStruct((t, LANES), F32),
            jax.ShapeDtypeStruct((rows, LANES), F32),
        ],
        scratch_shapes=[pltpu.VMEM((tm, d), BF16), pltpu.VMEM((rows, LANES), F32)],
        compiler_params=_cparams("arbitrary"),
        name="router",
    )(x, g, w, b)


def _slots_kernel(ps_ref, eid_ref, rank_ref, slot_ref, *, n_exp):
    eid = eid_ref[...]
    base = jnp.zeros_like(eid)
    for e in range(n_exp):
        base = jnp.where(eid == e, ps_ref[e], base)
    slot_ref[...] = base + rank_ref[...]


def _slots(pstarts, eid, rank):
    rows, t = eid.shape
    tm = _pick(t, (4096, 2048, 1024, 512, 256, 128))
    return pl.pallas_call(
        functools.partial(_slots_kernel, n_exp=pstarts.shape[0]),
        grid_spec=pltpu.PrefetchScalarGridSpec(
            num_scalar_prefetch=1,
            grid=(t // tm,),
            in_specs=[
                pl.BlockSpec((rows, tm), lambda i, ps: (0, i)),
                pl.BlockSpec((rows, tm), lambda i, ps: (0, i)),
            ],
            out_specs=pl.BlockSpec((rows, tm), lambda i, ps: (0, i)),
        ),
        out_shape=jax.ShapeDtypeStruct((rows, t), I32),
        compiler_params=_cparams("parallel"),
        name="moe_slots",
    )(pstarts, eid, rank)


def _dispatch_kernel(zb_ref, d0_ref, d1_ref, x_ref, g_ref, xb_ref, pk_ref, zero_ref, sem, zsem):
    i = pl.program_id(0)
    last = pl.num_programs(0) - 1
    tm, d = x_ref.shape
    half = d // 2
    slot = i % 2
    g = g_ref[...]

    @pl.when(i == 0)
    def _():
        zero_ref[...] = jnp.zeros_like(zero_ref)

        def zero_copy(j):
            start = pl.multiple_of(jnp.maximum(zb_ref[j], 0) * MOE_BLOCK, MOE_BLOCK)
            return pltpu.make_async_copy(zero_ref, xb_ref.at[pl.ds(start, MOE_BLOCK)], zsem.at[0])

        def zstart(j, c):
            @pl.when(zb_ref[j] >= 0)
            def _():
                zero_copy(j).start()
            return c

        def zwait(j, c):
            @pl.when(zb_ref[j] >= 0)
            def _():
                zero_copy(j).wait()
            return c

        lax.fori_loop(0, zb_ref.shape[0], zstart, 0)
        lax.fori_loop(0, zb_ref.shape[0], zwait, 0)

    def pack(j, c):
        r = pl.multiple_of(j * ROW_CHUNK, ROW_CHUNK)
        x = x_ref[pl.ds(r, ROW_CHUNK), :]
        ms = jnp.mean(x * x, axis=-1, keepdims=True)
        hn = (x * lax.rsqrt(ms + EPS) * g).astype(BF16).astype(F32)
        bits = lax.bitcast_convert_type(hn, U32)
        lo = lax.shift_right_logical(bits[:, :half], jnp.uint32(16))
        hi = bits[:, half:] & jnp.uint32(0xFFFF0000)
        pk_ref[slot, pl.ds(r, ROW_CHUNK), :] = hi | lo
        return c

    lax.fori_loop(0, tm // ROW_CHUNK, pack, 0)

    def issue(j, c):
        for u in range(DMA_UNROLL):
            r = j * DMA_UNROLL + u
            src = pk_ref.at[slot, pl.ds(r, 1)]
            pltpu.make_async_copy(src, xb_ref.at[pl.ds(d0_ref[r], 1)], sem.at[slot]).start(priority=u % 2)
            pltpu.make_async_copy(src, xb_ref.at[pl.ds(d1_ref[r], 1)], sem.at[slot]).start(priority=(u + 1) % 2)
        return c

    lax.fori_loop(0, tm // DMA_UNROLL, issue, 0)

    def drain(s):
        for _ in range(TOP_K):
            pltpu.make_async_copy(pk_ref.at[s], xb_ref.at[pl.ds(0, tm)], sem.at[s]).wait()

    @pl.when(i > 0)
    def _():
        drain(1 - slot)

    @pl.when(i == last)
    def _():
        drain(slot)


def _dispatch(zero_blocks, d0, d1, x, g, n_rows):
    t, d = x.shape
    tm = _pick(t, (256, 128))
    return pl.pallas_call(
        _dispatch_kernel,
        grid_spec=pltpu.PrefetchScalarGridSpec(
            num_scalar_prefetch=1,
            grid=(t // tm,),
            in_specs=[
                pl.BlockSpec((tm,), lambda i, zb: (i,), memory_space=pltpu.SMEM),
                pl.BlockSpec((tm,), lambda i, zb: (i,), memory_space=pltpu.SMEM),
                pl.BlockSpec((tm, d), lambda i, zb: (i, 0)),
                pl.BlockSpec((1, d), lambda i, zb: (0, 0)),
            ],
            out_specs=pl.BlockSpec(memory_space=pl.ANY),
            scratch_shapes=[pltpu.VMEM((2, tm, d // 2), U32), pltpu.VMEM((MOE_BLOCK, d // 2), U32),
                            pltpu.SemaphoreType.DMA((2,)), pltpu.SemaphoreType.DMA((1,))],
        ),
        out_shape=jax.ShapeDtypeStruct((n_rows, d // 2), U32),
        compiler_params=_cparams("arbitrary"),
        name="moe_dispatch",
    )(zero_blocks, d0, d1, x, g)


def _experts_kernel(be_ref, nu_ref, xb_ref, wg_ref, wu_ref, wd_ref, yb_ref):
    del be_ref
    half = xb_ref.shape[1]

    @pl.when(pl.program_id(0) < nu_ref[0])
    def _():
        w = xb_ref[...]
        x_lo = lax.bitcast_convert_type(lax.shift_left(w, jnp.uint32(16)), F32).astype(BF16)
        x_hi = lax.bitcast_convert_type(w & jnp.uint32(0xFFFF0000), F32).astype(BF16)
        h1 = (jnp.dot(x_lo, wg_ref[:half, :], preferred_element_type=F32)
              + jnp.dot(x_hi, wg_ref[half:, :], preferred_element_type=F32))
        h2 = (jnp.dot(x_lo, wu_ref[:half, :], preferred_element_type=F32)
              + jnp.dot(x_hi, wu_ref[half:, :], preferred_element_type=F32))
        hid = (h1 * _sigmoid(h1)) * h2
        yb_ref[...] = jnp.dot(hid.astype(BF16), wd_ref[...], preferred_element_type=F32)

    @pl.when(pl.program_id(0) >= nu_ref[0])
    def _():
        yb_ref[...] = jnp.zeros_like(yb_ref)


def _experts(blk_exp, n_used, xb, wg, wu, wd):
    p, half = xb.shape
    d = 2 * half
    de = wg.shape[-1]
    n_blk = p // MOE_BLOCK

    def live(i, nu):
        return jnp.minimum(i, nu[0] - 1)

    return pl.pallas_call(
        _experts_kernel,
        grid_spec=pltpu.PrefetchScalarGridSpec(
            num_scalar_prefetch=2,
            grid=(n_blk,),
            in_specs=[
                pl.BlockSpec((MOE_BLOCK, half), lambda i, be, nu: (live(i, nu), 0)),
                pl.BlockSpec((None, d, de), lambda i, be, nu: (be[live(i, nu)], 0, 0)),
                pl.BlockSpec((None, d, de), lambda i, be, nu: (be[live(i, nu)], 0, 0)),
                pl.BlockSpec((None, de, d), lambda i, be, nu: (be[live(i, nu)], 0, 0)),
            ],
            out_specs=pl.BlockSpec((MOE_BLOCK, d), lambda i, be, nu: (i, 0)),
        ),
        out_shape=jax.ShapeDtypeStruct((p, d), F32),
        compiler_params=_cparams("arbitrary"),
        name="moe_experts",
    )(blk_exp, n_used, xb, wg, wu, wd)


def _combine_kernel(*refs, final):
    if final:
        (d0_ref, d1_ref, d0n_ref, d1n_ref, x_ref, gate_ref, yb_ref, gf_ref,
         o_ref, a_ref, b_ref, sem) = refs
    else:
        d0_ref, d1_ref, d0n_ref, d1n_ref, x_ref, gate_ref, yb_ref, o_ref, a_ref, b_ref, sem = refs
    i = pl.program_id(0)
    last = pl.num_programs(0) - 1
    tm = x_ref.shape[0]
    slot = i % 2

    def gather(s0_ref, s1_ref, s):
        def issue(j, c):
            for u in range(DMA_UNROLL):
                r = j * DMA_UNROLL + u
                pltpu.make_async_copy(yb_ref.at[pl.ds(s0_ref[r], 1)], a_ref.at[s, pl.ds(r, 1)],
                                      sem.at[0, s]).start(priority=u % 2)
                pltpu.make_async_copy(yb_ref.at[pl.ds(s1_ref[r], 1)], b_ref.at[s, pl.ds(r, 1)],
                                      sem.at[1, s]).start(priority=(u + 1) % 2)
            return c

        lax.fori_loop(0, tm // DMA_UNROLL, issue, 0)

    @pl.when(i == 0)
    def _():
        gather(d0_ref, d1_ref, 0)

    @pl.when(i < last)
    def _():
        gather(d0n_ref, d1n_ref, 1 - slot)

    pltpu.make_async_copy(yb_ref.at[pl.ds(0, tm)], a_ref.at[slot], sem.at[0, slot]).wait()
    pltpu.make_async_copy(yb_ref.at[pl.ds(0, tm)], b_ref.at[slot], sem.at[1, slot]).wait()

    def body(j, c):
        r = pl.multiple_of(j * ROW_CHUNK, ROW_CHUNK)
        rs = pl.ds(r, ROW_CHUNK)
        gates = gate_ref[rs, :]
        y = gates[:, 0:1] * a_ref[slot, rs, :] + gates[:, 1:2] * b_ref[slot, rs, :]
        xn = x_ref[rs, :] + y
        if final:
            ms = jnp.mean(xn * xn, axis=-1, keepdims=True)
            xn = xn * lax.rsqrt(ms + EPS) * gf_ref[...]
        o_ref[rs, :] = xn
        return c

    lax.fori_loop(0, tm // ROW_CHUNK, body, 0)


def _combine(d0, d1, x, gates, yb, g_final):
    t, d = x.shape
    tm = _pick(t, (256, 128))
    n = t // tm
    final = g_final is not None

    def nxt(i):
        return (jnp.minimum(i + 1, n - 1),)

    in_specs = [
        pl.BlockSpec((tm,), lambda i: (i,), memory_space=pltpu.SMEM),
        pl.BlockSpec((tm,), lambda i: (i,), memory_space=pltpu.SMEM),
        pl.BlockSpec((tm,), nxt, memory_space=pltpu.SMEM),
        pl.BlockSpec((tm,), nxt, memory_space=pltpu.SMEM),
        pl.BlockSpec((tm, d), lambda i: (i, 0)),
        pl.BlockSpec((tm, LANES), lambda i: (i, 0)),
        pl.BlockSpec(memory_space=pl.ANY),
    ]
    args = [d0, d1, d0, d1, x, gates, yb]
    if final:
        in_specs.append(pl.BlockSpec((1, d), lambda i: (0, 0)))
        args.append(g_final)
    return pl.pallas_call(
        functools.partial(_combine_kernel, final=final),
        grid=(n,),
        in_specs=in_specs,
        out_specs=pl.BlockSpec((tm, d), lambda i: (i, 0)),
        out_shape=jax.ShapeDtypeStruct((t, d), F32),
        scratch_shapes=[pltpu.VMEM((2, tm, d), F32), pltpu.VMEM((2, tm, d), F32),
                        pltpu.SemaphoreType.DMA((2, 2))],
        compiler_params=_cparams("arbitrary"),
        name="moe_combine",
    )(*args)


def _moe(x, g_norm, w_route, b_route, wg, wu, wd, g_final, *, n_grp, epg):
    t, d = x.shape
    n_exp = n_grp * epg
    eid, rank, gates, cnt = _router(x, g_norm, w_route, b_route, n_grp=n_grp, epg=epg)

    counts = cnt[n_grp:n_grp + n_exp, 0].astype(I32)
    pblocks = (counts + MOE_BLOCK - 1) // MOE_BLOCK
    bends = jnp.cumsum(pblocks)
    pstarts = (bends - pblocks) * MOE_BLOCK
    n_blk = (t * TOP_K) // MOE_BLOCK + n_exp
    blk = jnp.arange(n_blk, dtype=I32)
    blk_exp = jnp.minimum(jnp.sum(bends[None, :] <= blk[:, None], axis=1), n_exp - 1).astype(I32)
    n_used = bends[-1:].astype(I32)
    last_blk = jnp.where(counts > 0, bends - 1, -1)
    trail = n_used + jnp.arange(n_exp, dtype=I32)
    zero_blocks = jnp.concatenate([last_blk, jnp.where(trail < n_blk, trail, -1)]).astype(I32)

    slots = _slots(pstarts.astype(I32), eid, rank)
    d0, d1 = slots[0], slots[1]
    xb = _dispatch(zero_blocks, d0, d1, x, g_norm, n_blk * MOE_BLOCK)
    yb = _experts(blk_exp, n_used, xb, wg, wu, wd)
    return _combine(d0, d1, x, gates, yb, g_final)


def _row(v):
    return v.reshape(1, -1).astype(F32)


def _trunk(x, p):
    b, s, d = x.shape
    t = b * s
    depth = p["norm_mix"].shape[0]
    xt = x.reshape(t, d)
    for i in range(depth):
        j = i // 2
        if i % 2 == 0:
            z, gates = _inproj_mlstm(xt, _row(p["norm_mix"][i]), p["mlstm_w_main"][j], p["mlstm_w_gate"][j])
            ng = 4 * MLSTM_HEADS
            gcol = gates.reshape(b, s, N_GATE_LANES)
            grow = jnp.swapaxes(gcol[:, :, :ng], 1, 2)
            brow, bcol = p["mlstm_b_row"][j], p["mlstm_b_col"][j]
            z3 = z.reshape(b, s, -1)
            hf = _mlstm(z3, gcol, grow, brow, bcol, reverse=False)
            hn = _mlstm(z3, gcol, grow, brow, bcol, reverse=True, fused=(hf, _row(p["mlstm_g_hnorm"][j])))
            xt = _outproj(hn.reshape(t, -1), p["mlstm_w_out"][j], xt)
        else:
            z, ssq = _inproj_sgu(xt, _row(p["norm_mix"][i]), p["sgu_w_in"][j])
            xt = _sgu_mix(z, ssq, _row(p["sgu_g_v"][j]), p["sgu_w_s"][j], p["sgu_b_s"][j],
                          p["sgu_w_out"][j], xt)
        g_final = _row(p["norm_final"]) if i == depth - 1 else None
        xt = _moe(xt, _row(p["norm_ffn"][i]), p["moe_w_route"][i], p["moe_b_route"][i],
                  p["moe_w_gate"][i], p["moe_w_up"][i], p["moe_w_down"][i], g_final,
                  n_grp=p["n_grp"], epg=p["epg"])
    if depth == 0:
        raise ValueError("depth 0 is not supported")
    return xt.reshape(b, s, d)


def kernel(x_prompt, x_sample, norm_mix, norm_ffn, norm_final, mlstm_w_in, mlstm_b_gates, mlstm_g_hnorm, mlstm_w_out, sgu_w_in, sgu_g_v, sgu_w_s, sgu_b_s, sgu_w_out, moe_w_grp, moe_b_grp, moe_w_exp, moe_b_exp, moe_w_gate, moe_w_up, moe_w_down):
    ng = 4 * MLSTM_HEADS
    n_main = mlstm_w_in.shape[-1] - ng
    n_grp = moe_w_grp.shape[-1]
    n_exp = moe_w_exp.shape[-1]
    assert n_grp + n_exp <= LANES and ng <= N_GATE_LANES

    def pad_lanes(a, width):
        return jnp.pad(a, [(0, 0)] * (a.ndim - 1) + [(0, width - a.shape[-1])])

    p = dict(
        norm_mix=norm_mix, norm_ffn=norm_ffn, norm_final=norm_final,
        mlstm_w_main=mlstm_w_in[..., :n_main].astype(BF16),
        mlstm_w_gate=pad_lanes(mlstm_w_in[..., n_main:], N_GATE_LANES).astype(BF16),
        mlstm_b_row=pad_lanes(mlstm_b_gates, N_GATE_LANES)[:, None, :].astype(F32),
        mlstm_b_col=jnp.broadcast_to(mlstm_b_gates[:, :, None], mlstm_b_gates.shape + (MLSTM_CHUNK,)).astype(F32),
        mlstm_g_hnorm=mlstm_g_hnorm,
        mlstm_w_out=mlstm_w_out.astype(BF16),
        sgu_w_in=sgu_w_in.astype(BF16),
        sgu_g_v=sgu_g_v,
        sgu_w_s=sgu_w_s.astype(BF16),
        sgu_b_s=jnp.broadcast_to(sgu_b_s[..., None], sgu_b_s.shape + (LANES,)).astype(F32),
        sgu_w_out=sgu_w_out.astype(BF16),
        moe_w_route=pad_lanes(jnp.concatenate([moe_w_grp, moe_w_exp], axis=-1), LANES).astype(BF16),
        moe_b_route=pad_lanes(jnp.concatenate([moe_b_grp, moe_b_exp], axis=-1), LANES)[:, None, :].astype(F32),
        moe_w_gate=moe_w_gate.astype(BF16),
        moe_w_up=moe_w_up.astype(BF16),
        moe_w_down=moe_w_down.astype(BF16),
        n_grp=n_grp, epg=n_exp // n_grp,
    )
    return (_trunk(x_prompt, p), _trunk(x_sample, p))
```

```python
import functools

import jax
import jax.numpy as jnp
import numpy as np
from jax import lax
from jax.experimental import pallas as pl
from jax.experimental.pallas import tpu as pltpu

F32 = jnp.float32
BF16 = jnp.bfloat16
U32 = jnp.uint32
I32 = jnp.int32

EPS = 1e-6
MLSTM_HEADS = 4
MLSTM_CHUNK = 128
GATE_SOFTCAP = 15.0
N_GATE_LANES = 128
SGU_CHUNK = 128
TOP_K = 2
ROW_CHUNK = 128
MOE_BLOCK = 512
DMA_UNROLL = 8
LANES = 128
VMEM_LIMIT_BYTES = 56 * 1024 * 1024


def _cparams(*sem, **kw):
    return pltpu.CompilerParams(dimension_semantics=sem, vmem_limit_bytes=VMEM_LIMIT_BYTES, **kw)


def _pick(n, prefs):
    for p in prefs:
        if n % p == 0:
            return p
    return n


def _rmsnorm_rows(x_ref, g_ref, out_ref):
    rows = x_ref.shape[0]
    g = g_ref[...]

    def body(i, c):
        r = pl.multiple_of(i * ROW_CHUNK, ROW_CHUNK)
        x = x_ref[pl.ds(r, ROW_CHUNK), :]
        ms = jnp.mean(x * x, axis=-1, keepdims=True)
        out_ref[pl.ds(r, ROW_CHUNK), :] = (x * lax.rsqrt(ms + EPS) * g).astype(out_ref.dtype)
        return c

    lax.fori_loop(0, rows // ROW_CHUNK, body, 0)


def _sigmoid(x):
    return 1.0 / (1.0 + jnp.exp(-x))


def _log_sigmoid(x):
    return jnp.minimum(x, 0.0) - jnp.log1p(jnp.exp(-jnp.abs(x)))


def _soft_cap(t):
    return GATE_SOFTCAP * jnp.tanh(t / GATE_SOFTCAP)


def _split3(a):
    hi = a.astype(BF16)
    r1 = a - hi.astype(F32)
    mid = r1.astype(BF16)
    lo = (r1 - mid.astype(F32)).astype(BF16)
    return [hi, mid, lo]


def _resident(shape):
    return pl.BlockSpec(shape, lambda i: (0,) * len(shape), pipeline_mode=pl.Buffered(1))


def _col_chunks(n):
    step = _pick(n, (1024, 512, 256, 128))
    return [(c, step) for c in range(0, n, step)]


def _inproj_mlstm_kernel(x_ref, g_ref, w_ref, wkt_ref, wg_ref, z_ref, kt_ref, gate_ref, xn_ref):
    _rmsnorm_rows(x_ref, g_ref, xn_ref)
    xn = xn_ref[...]
    gate_ref[...] = jnp.dot(xn, wg_ref[...], preferred_element_type=F32)
    kt_ref[...] = lax.dot_general(wkt_ref[...], xn, (((1,), (1,)), ((), ())),
                                  preferred_element_type=F32).astype(kt_ref.dtype)
    for c, w in _col_chunks(w_ref.shape[1]):
        z_ref[:, c:c + w] = jnp.dot(xn, w_ref[:, c:c + w], preferred_element_type=F32).astype(z_ref.dtype)


def _inproj_mlstm(x, g, w, wkt, wg):
    t, d = x.shape
    n = w.shape[1]
    qk = wkt.shape[0]
    tm = _pick(t, (512, 256, 128))
    return pl.pallas_call(
        _inproj_mlstm_kernel,
        grid=(t // tm,),
        in_specs=[
            pl.BlockSpec((tm, d), lambda i: (i, 0)),
            _resident((1, d)),
            _resident((d, n)),
            _resident((qk, d)),
            _resident((d, N_GATE_LANES)),
        ],
        out_specs=[
            pl.BlockSpec((tm, n), lambda i: (i, 0)),
            pl.BlockSpec((qk, tm), lambda i: (0, i)),
            pl.BlockSpec((tm, N_GATE_LANES), lambda i: (i, 0)),
        ],
        out_shape=[
            jax.ShapeDtypeStruct((t, n), BF16),
            jax.ShapeDtypeStruct((qk, t), BF16),
            jax.ShapeDtypeStruct((t, N_GATE_LANES), F32),
        ],
        scratch_shapes=[pltpu.VMEM((tm, d), BF16)],
        compiler_params=_cparams("parallel"),
        name="inproj_mlstm",
    )(x, g, w, wkt, wg)


def _mlstm_kernel(*refs, reverse, fuse, heads, dk, dv):
    n_in = 11 if fuse else 8
    if fuse:
        (q_ref, kt_ref, v_ref, gcol_ref, grow_ref, brow_ref, bcol_ref, sel_ref,
         o_ref, hf_ref, gh_ref) = refs[:n_in]
    else:
        q_ref, kt_ref, v_ref, gcol_ref, grow_ref, brow_ref, bcol_ref, sel_ref = refs[:n_in]
    out_ref = refs[n_in]
    state = refs[n_in + 1:]
    c_refs, m_refs = state[0::2], state[1::2]
    L = MLSTM_CHUNK

    @pl.when(pl.program_id(1) == 0)
    def _():
        for ref in state:
            ref[...] = jnp.zeros_like(ref)

    gc = _soft_cap(gcol_ref[...] + brow_ref[...])
    gr = _soft_cap(grow_ref[...] + bcol_ref[...])
    lsc = _log_sigmoid(gc)
    lsr = _log_sigmoid(gr)
    row = lax.broadcasted_iota(I32, (L, L), 0)
    col = lax.broadcasted_iota(I32, (L, L), 1)
    lower = col <= row
    upper = col >= row
    mask_t, mask_s = (upper, lower) if reverse else (lower, upper)
    ng = gr.shape[0]
    incl_t = jnp.where(mask_t, 1.0, 0.0).astype(BF16)
    incl_s = jnp.where(mask_s, 1.0, 0.0).astype(BF16)
    cc = jnp.dot(incl_t, jnp.concatenate(_split3(lsc), axis=1), preferred_element_type=F32)
    cum_cols = cc[:, :LANES] + cc[:, LANES:2 * LANES] + cc[:, 2 * LANES:]
    cr = jnp.dot(jnp.concatenate(_split3(lsr), axis=0), incl_s, preferred_element_type=F32)
    cum_rows = cr[:ng] + cr[ng:2 * ng] + cr[2 * ng:]
    spread = jnp.dot(jnp.concatenate(_split3(gc) + _split3(cum_cols), axis=1), sel_ref[...],
                     preferred_element_type=F32)
    end = 0 if reverse else L - 1
    scale = dk ** -0.5
    reps = dv // LANES

    def lanes(a, n):
        return jnp.concatenate([a] * n, axis=1) if n > 1 else a

    stage1 = []
    for h in range(heads):
        ci = (2 * heads if reverse else 0) + h
        cf = ci + heads
        i_col = spread[:, 2 * h * LANES:(2 * h + 1) * LANES]
        b_col = spread[:, (2 * h + 1) * LANES:(2 * h + 2) * LANES]
        i_row = gr[ci:ci + 1, :]
        b_row = cum_rows[cf:cf + 1, :]
        g = b_col[end:end + 1, :]
        m_prev = m_refs[h][...]
        a_col = g - b_col + i_col
        m_new = jnp.maximum(g + m_prev, jnp.max(a_col, axis=0, keepdims=True))
        decay = jnp.exp(g + m_prev - m_new)
        w_row = jnp.exp(g - b_row + i_row - m_new)

        q = q_ref[:, h * dk:(h + 1) * dk]
        kt = kt_ref[h * dk:(h + 1) * dk, :]
        qs_b = (q.astype(F32) * scale).astype(BF16)
        cx_old = c_refs[h][...]
        num_x = jnp.dot(qs_b, cx_old.astype(BF16), preferred_element_type=F32)
        qk = jnp.dot(qs_b, kt, preferred_element_type=F32)

        log_inter = b_col + m_prev
        dlog = jnp.where(mask_t, b_col - b_row + i_row, -jnp.inf)
        m_t = jnp.maximum(log_inter, jnp.max(dlog, axis=1, keepdims=True))
        stage1.append((m_new, decay, w_row, kt, cx_old, num_x, qk, log_inter, dlog, m_t))

    ones = jnp.ones((L, LANES), BF16)
    for h in range(heads):
        m_new, decay, w_row, kt, cx_old, num_x, qk, log_inter, dlog, m_t = stage1[h]
        vx = jnp.concatenate([v_ref[:, h * dv:(h + 1) * dv], ones], axis=1)
        s = qk * jnp.exp(dlog - m_t)
        inter = jnp.exp(log_inter - m_t)
        sv = jnp.dot(s.astype(BF16), vx, preferred_element_type=F32)
        num = lanes(inter, reps) * num_x[:, :dv] + sv[:, :dv]
        den = inter * num_x[:, dv:] + sv[:, dv:]
        hh = num * lanes(1.0 / jnp.maximum(jnp.abs(den), jnp.exp(-m_t)), reps)

        kwt = (kt.astype(F32) * w_row).astype(BF16)
        c_refs[h][...] = lanes(decay, reps + 1) * cx_old + jnp.dot(kwt, vx, preferred_element_type=F32)
        m_refs[h][...] = m_new

        hs = slice(h * dv, (h + 1) * dv)
        if fuse:
            tot = hf_ref[:, hs] + hh
            ms = jnp.mean(tot * tot, axis=-1, keepdims=True)
            gated = tot * lax.rsqrt(ms + EPS) * gh_ref[:, hs] * _sigmoid(o_ref[:, hs].astype(F32))
            out_ref[:, hs] = gated.astype(out_ref.dtype)
        else:
            out_ref[:, hs] = hh


def _mlstm(z, kt, gcol, grow, brow, bcol, *, reverse, fused=None):
    b, s, width = z.shape
    heads = MLSTM_HEADS
    qk = kt.shape[0]
    vdim = (width - qk) // 2
    dk, dv = qk // heads, vdim // heads
    nc = s // MLSTM_CHUNK
    L = MLSTM_CHUNK
    ng = grow.shape[1]
    assert L == N_GATE_LANES and dv % LANES == 0 and (2 * vdim) % qk == 0

    def ch(c):
        return nc - 1 - c if reverse else c

    in_specs = [
        pl.BlockSpec((None, L, qk), lambda bi, c: (bi, ch(c), (2 * vdim) // qk)),
        pl.BlockSpec((qk, L), lambda bi, c: (0, bi * nc + ch(c))),
        pl.BlockSpec((None, L, vdim), lambda bi, c: (bi, ch(c), 0)),
        pl.BlockSpec((None, L, N_GATE_LANES), lambda bi, c: (bi, ch(c), 0)),
        pl.BlockSpec((None, ng, L), lambda bi, c: (bi, 0, ch(c))),
        pl.BlockSpec((1, N_GATE_LANES), lambda bi, c: (0, 0)),
        pl.BlockSpec((ng, L), lambda bi, c: (0, 0)),
        pl.BlockSpec((6 * LANES, 2 * heads * LANES), lambda bi, c: (0, 0)),
    ]
    sel = np.zeros((6, LANES, 2 * heads, LANES), np.float32)
    for h in range(heads):
        ci = (2 * heads if reverse else 0) + h
        sel[0:3, ci, 2 * h, :] = 1.0
        sel[3:6, ci + heads, 2 * h + 1, :] = 1.0
    sel = jnp.asarray(sel.reshape(6 * LANES, 2 * heads * LANES), BF16)
    args = [z, kt, z, gcol, grow, brow, bcol, sel]
    if fused is not None:
        hf, gh = fused
        in_specs += [
            pl.BlockSpec((None, L, vdim), lambda bi, c: (bi, ch(c), 1)),
            pl.BlockSpec((None, L, vdim), lambda bi, c: (bi, ch(c), 0)),
            pl.BlockSpec((1, vdim), lambda bi, c: (0, 0)),
        ]
        args += [z, hf, gh]
    return pl.pallas_call(
        functools.partial(_mlstm_kernel, reverse=reverse, fuse=fused is not None,
                          heads=heads, dk=dk, dv=dv),
        grid=(b, nc),
        in_specs=in_specs,
        out_specs=pl.BlockSpec((None, L, vdim), lambda bi, c: (bi, ch(c), 0)),
        out_shape=jax.ShapeDtypeStruct((b, s, vdim), BF16 if fused is not None else F32),
        scratch_shapes=[pltpu.VMEM((dk, dv + LANES), F32), pltpu.VMEM((1, LANES), F32)] * heads,
        compiler_params=_cparams("parallel", "arbitrary"),
        name="mlstm_bwd" if reverse else "mlstm_fwd",
    )(*args)


def _outproj_kernel(h_ref, w_ref, x_ref, o_ref):
    h = h_ref[...]
    for c, w in _col_chunks(w_ref.shape[1]):
        o_ref[:, c:c + w] = x_ref[:, c:c + w] + jnp.dot(h, w_ref[:, c:c + w], preferred_element_type=F32)


def _outproj(h, w, x):
    t, kdim = h.shape
    n = w.shape[1]
    tm = _pick(t, (512, 256, 128))
    return pl.pallas_call(
        _outproj_kernel,
        grid=(t // tm,),
        in_specs=[
            pl.BlockSpec((tm, kdim), lambda i: (i, 0)),
            _resident((kdim, n)),
            pl.BlockSpec((tm, n), lambda i: (i, 0)),
        ],
        out_specs=pl.BlockSpec((tm, n), lambda i: (i, 0)),
        out_shape=jax.ShapeDtypeStruct((t, n), F32),
        compiler_params=_cparams("parallel"),
        name="outproj",
    )(h, w, x)


def _inproj_sgu_kernel(x_ref, g_ref, w_ref, z_ref, ssq_ref, xn_ref):
    _rmsnorm_rows(x_ref, g_ref, xn_ref)
    xn = xn_ref[...]
    n = w_ref.shape[1]
    part = None
    for c, w in _col_chunks(n // 2) + [(n // 2 + c, w) for c, w in _col_chunks(n // 2)]:
        z = jax.nn.gelu(jnp.dot(xn, w_ref[:, c:c + w], preferred_element_type=F32))
        z_ref[:, c:c + w] = z.astype(z_ref.dtype)
        if c >= n // 2:
            z2 = z * z
            for l in range(0, w, LANES):
                part = z2[:, l:l + LANES] if part is None else part + z2[:, l:l + LANES]
    ssq_ref[...] = part


def _inproj_sgu(x, g, w):
    t, d = x.shape
    n = w.shape[1]
    tm = _pick(t, (512, 256, 128))
    return pl.pallas_call(
        _inproj_sgu_kernel,
        grid=(t // tm,),
        in_specs=[
            pl.BlockSpec((tm, d), lambda i: (i, 0)),
            _resident((1, d)),
            _resident((d, n)),
        ],
        out_specs=[
            pl.BlockSpec((tm, n), lambda i: (i, 0)),
            pl.BlockSpec((tm, LANES), lambda i: (i, 0)),
        ],
        out_shape=[
            jax.ShapeDtypeStruct((t, n), BF16),
            jax.ShapeDtypeStruct((t, LANES), F32),
        ],
        scratch_shapes=[pltpu.VMEM((tm, d), BF16)],
        compiler_params=_cparams("parallel"),
        name="inproj_sgu",
    )(x, g, w)


def _sgu_mix_kernel(u_ref, v_ref, ssq_ref, gv_ref, ws_ref, bs_ref, wo_ref, x_ref, o_ref, gated_ref,
                    *, groups, gd):
    rows = u_ref.shape[0]
    width = v_ref.shape[1]
    gv = gv_ref[...]
    for c in range(rows // SGU_CHUNK):
        rs = slice(c * SGU_CHUNK, (c + 1) * SGU_CHUNK)
        ms = jnp.sum(ssq_ref[rs, :], axis=-1, keepdims=True) / width
        vn = (v_ref[rs, :].astype(F32) * lax.rsqrt(ms + EPS) * gv).astype(BF16)
        for gi in range(groups):
            cs = slice(gi * gd, (gi + 1) * gd)
            mixed = jnp.dot(ws_ref[gi], vn[:, cs], preferred_element_type=F32)
            bias = bs_ref[gi]
            bias = jnp.concatenate([bias] * (gd // LANES), axis=1) if gd > LANES else bias
            gated_ref[rs, cs] = (u_ref[rs, cs].astype(F32) * (mixed + bias)).astype(BF16)
    gated = gated_ref[...]
    for c, w in _col_chunks(wo_ref.shape[1]):
        o_ref[:, c:c + w] = x_ref[:, c:c + w] + jnp.dot(gated, wo_ref[:, c:c + w], preferred_element_type=F32)


def _sgu_mix(z, ssq, gv, ws, bs, wo, x):
    t, d = x.shape
    width = z.shape[1] // 2
    groups = ws.shape[0]
    tm = _pick(t, (512, 256, 128))
    return pl.pallas_call(
        functools.partial(_sgu_mix_kernel, groups=groups, gd=width // groups),
        grid=(t // tm,),
        in_specs=[
            pl.BlockSpec((tm, width), lambda i: (i, 0)),
            pl.BlockSpec((tm, width), lambda i: (i, 1)),
            pl.BlockSpec((tm, LANES), lambda i: (i, 0)),
            _resident((1, width)),
            _resident((groups, SGU_CHUNK, SGU_CHUNK)),
            _resident((groups, SGU_CHUNK, LANES)),
            _resident((width, d)),
            pl.BlockSpec((tm, d), lambda i: (i, 0)),
        ],
        out_specs=pl.BlockSpec((tm, d), lambda i: (i, 0)),
        out_shape=jax.ShapeDtypeStruct((t, d), F32),
        scratch_shapes=[pltpu.VMEM((tm, width), BF16)],
        compiler_params=_cparams("parallel"),
        name="sgu_mix",
    )(z, z, ssq, gv, ws, bs, wo, x)


def _router_kernel(x_ref, g_ref, w_ref, b_ref, eid_ref, rank_ref, gate_ref, cnt_ref, hn_ref, carry_ref,
                   *, n_grp, epg, rows):
    tm = x_ref.shape[0]

    @pl.when(pl.program_id(0) == 0)
    def _():
        carry_ref[...] = jnp.zeros_like(carry_ref)

    _rmsnorm_rows(x_ref, g_ref, hn_ref)
    logits = jnp.dot(hn_ref[...], w_ref[...], preferred_element_type=F32) + b_ref[...]
    lt = logits.T[:rows, :]
    ridx = lax.broadcasted_iota(I32, (rows, tm), 0).astype(F32)
    big = float(rows)

    gl = jnp.where(ridx < n_grp, lt, -jnp.inf)
    gmax = jnp.max(gl, axis=0, keepdims=True)
    grp = jnp.min(jnp.where(gl == gmax, ridx, big), axis=0, keepdims=True)
    grp_p = 1.0 / jnp.sum(jnp.exp(gl - gmax), axis=0, keepdims=True)

    lo = n_grp + grp * epg
    el = jnp.where(ridx >= lo, jnp.where(ridx < lo + epg, lt, -jnp.inf), -jnp.inf)
    emax = jnp.max(el, axis=0, keepdims=True)
    ee = jnp.exp(el - emax)
    p = ee / jnp.sum(ee, axis=0, keepdims=True)
    pm = jnp.where(ridx >= lo, jnp.where(ridx < lo + epg, p, -1.0), -1.0)
    p1 = jnp.max(pm, axis=0, keepdims=True)
    i1 = jnp.min(jnp.where(pm == p1, ridx, big), axis=0, keepdims=True)
    pm2 = jnp.where(ridx == i1, -1.0, pm)
    p2 = jnp.max(pm2, axis=0, keepdims=True)
    i2 = jnp.min(jnp.where(pm2 == p2, ridx, big), axis=0, keepdims=True)
    psum = p1 + p2
    g1 = grp_p * (p1 / psum)
    g2 = grp_p * (p2 / psum)

    hit1 = ridx == i1
    hit2 = ridx == i2
    onehot = jnp.where(hit1, 1.0, jnp.where(hit2, 1.0, 0.0))
    r = lax.broadcasted_iota(I32, (tm, tm), 0)
    c = lax.broadcasted_iota(I32, (tm, tm), 1)
    earlier = jnp.where(r < c, 1.0, 0.0).astype(BF16)
    carry = carry_ref[:, 0:1]
    before = jnp.dot(onehot.astype(BF16), earlier, preferred_element_type=F32) + carry
    rank1 = jnp.sum(jnp.where(hit1, before, 0.0), axis=0, keepdims=True)
    rank2 = jnp.sum(jnp.where(hit2, before, 0.0), axis=0, keepdims=True)
    carry = carry + jnp.sum(onehot, axis=1, keepdims=True)
    carry_ref[...] = jnp.broadcast_to(carry, carry_ref.shape)
    cnt_ref[...] = jnp.broadcast_to(carry, cnt_ref.shape)

    sub = lax.broadcasted_iota(I32, (8, tm), 0)
    eid_ref[...] = jnp.where(sub == 0, i1 - n_grp, jnp.where(sub == 1, i2 - n_grp, 0.0)).astype(I32)
    rank_ref[...] = jnp.where(sub == 0, rank1, jnp.where(sub == 1, rank2, 0.0)).astype(I32)
    sub_l = lax.broadcasted_iota(I32, (LANES, tm), 0)
    gate_ref[...] = jnp.where(sub_l == 0, g1, jnp.where(sub_l == 1, g2, 0.0)).T


def _router(x, g, w, b, *, n_grp, epg):
    t, d = x.shape
    tm = _pick(t, (512, 256, 128))
    rows = -(-(n_grp + n_grp * epg) // 8) * 8
    return pl.pallas_call(
        functools.partial(_router_kernel, n_grp=n_grp, epg=epg, rows=rows),
        grid=(t // tm,),
        in_specs=[
            pl.BlockSpec((tm, d), lambda i: (i, 0)),
            pl.BlockSpec((1, d), lambda i: (0, 0)),
            pl.BlockSpec((d, LANES), lambda i: (0, 0)),
            pl.BlockSpec((1, LANES), lambda i: (0, 0)),
        ],
        out_specs=[
            pl.BlockSpec((8, tm), lambda i: (0, i)),
            pl.BlockSpec((8, tm), lambda i: (0, i)),
            pl.BlockSpec((tm, LANES), lambda i: (i, 0)),
            pl.BlockSpec((rows, LANES), lambda i: (0, 0)),
        ],
        out_shape=[
            jax.ShapeDtypeStruct((8, t), I32),
            jax.ShapeDtypeStruct((8, t), I32),
            jax.ShapeDtypeStruct((t, LANES), F32),
            jax.ShapeDtypeStruct((rows, LANES), F32),
        ],
        scratch_shapes=[pltpu.VMEM((tm, d), BF16), pltpu.VMEM((rows, LANES), F32)],
        compiler_params=_cparams("arbitrary"),
        name="router",
    )(x, g, w, b)


def _slots_kernel(ps_ref, eid_ref, rank_ref, slot_ref, *, n_exp):
    eid = eid_ref[...]
    base = jnp.zeros_like(eid)
    for e in range(n_exp):
        base = jnp.where(eid == e, ps_ref[e], base)
    slot_ref[...] = base + rank_ref[...]


def _slots(pstarts, eid, rank):
    rows, t = eid.shape
    tm = _pick(t, (4096, 2048, 1024, 512, 256, 128))
    return pl.pallas_call(
        functools.partial(_slots_kernel, n_exp=pstarts.shape[0]),
        grid_spec=pltpu.PrefetchScalarGridSpec(
            num_scalar_prefetch=1,
            grid=(t // tm,),
            in_specs=[
                pl.BlockSpec((rows, tm), lambda i, ps: (0, i)),
                pl.BlockSpec((rows, tm), lambda i, ps: (0, i)),
            ],
            out_specs=pl.BlockSpec((rows, tm), lambda i, ps: (0, i)),
        ),
        out_shape=jax.ShapeDtypeStruct((rows, t), I32),
        compiler_params=_cparams("parallel"),
        name="moe_slots",
    )(pstarts, eid, rank)


def _dispatch_kernel(zb_ref, d0_ref, d1_ref, x_ref, g_ref, xb_ref, pk_ref, zero_ref, sem, zsem):
    i = pl.program_id(0)
    last = pl.num_programs(0) - 1
    tm, d = x_ref.shape
    half = d // 2
    slot = i % 2
    g = g_ref[...]

    @pl.when(i == 0)
    def _():
        zero_ref[...] = jnp.zeros_like(zero_ref)

        def zero_copy(j):
            start = pl.multiple_of(jnp.maximum(zb_ref[j], 0) * MOE_BLOCK, MOE_BLOCK)
            return pltpu.make_async_copy(zero_ref, xb_ref.at[pl.ds(start, MOE_BLOCK)], zsem.at[0])

        def zstart(j, c):
            @pl.when(zb_ref[j] >= 0)
            def _():
                zero_copy(j).start()
            return c

        def zwait(j, c):
            @pl.when(zb_ref[j] >= 0)
            def _():
                zero_copy(j).wait()
            return c

        lax.fori_loop(0, zb_ref.shape[0], zstart, 0)
        lax.fori_loop(0, zb_ref.shape[0], zwait, 0)

    def pack(j, c):
        r = pl.multiple_of(j * ROW_CHUNK, ROW_CHUNK)
        x = x_ref[pl.ds(r, ROW_CHUNK), :]
        ms = jnp.mean(x * x, axis=-1, keepdims=True)
        hn = (x * lax.rsqrt(ms + EPS) * g).astype(BF16).astype(F32)
        bits = lax.bitcast_convert_type(hn, U32)
        lo = lax.shift_right_logical(bits[:, :half], jnp.uint32(16))
        hi = bits[:, half:] & jnp.uint32(0xFFFF0000)
        pk_ref[slot, pl.ds(r, ROW_CHUNK), :] = hi | lo
        return c

    lax.fori_loop(0, tm // ROW_CHUNK, pack, 0)

    def issue(j, c):
        for u in range(DMA_UNROLL):
            r = j * DMA_UNROLL + u
            src = pk_ref.at[slot, pl.ds(r, 1)]
            pltpu.make_async_copy(src, xb_ref.at[pl.ds(d0_ref[r], 1)], sem.at[slot]).start(priority=u % 2)
            pltpu.make_async_copy(src, xb_ref.at[pl.ds(d1_ref[r], 1)], sem.at[slot]).start(priority=(u + 1) % 2)
        return c

    lax.fori_loop(0, tm // DMA_UNROLL, issue, 0)

    def drain(s):
        for _ in range(TOP_K):
            pltpu.make_async_copy(pk_ref.at[s], xb_ref.at[pl.ds(0, tm)], sem.at[s]).wait()

    @pl.when(i > 0)
    def _():
        drain(1 - slot)

    @pl.when(i == last)
    def _():
        drain(slot)


def _dispatch(zero_blocks, d0, d1, x, g, n_rows):
    t, d = x.shape
    tm = _pick(t, (256, 128))
    return pl.pallas_call(
        _dispatch_kernel,
        grid_spec=pltpu.PrefetchScalarGridSpec(
            num_scalar_prefetch=1,
            grid=(t // tm,),
            in_specs=[
                pl.BlockSpec((tm,), lambda i, zb: (i,), memory_space=pltpu.SMEM),
                pl.BlockSpec((tm,), lambda i, zb: (i,), memory_space=pltpu.SMEM),
                pl.BlockSpec((tm, d), lambda i, zb: (i, 0)),
                pl.BlockSpec((1, d), lambda i, zb: (0, 0)),
            ],
            out_specs=pl.BlockSpec(memory_space=pl.ANY),
            scratch_shapes=[pltpu.VMEM((2, tm, d // 2), U32), pltpu.VMEM((MOE_BLOCK, d // 2), U32),
                            pltpu.SemaphoreType.DMA((2,)), pltpu.SemaphoreType.DMA((1,))],
        ),
        out_shape=jax.ShapeDtypeStruct((n_rows, d // 2), U32),
        compiler_params=_cparams("arbitrary", disable_bounds_checks=True),
        name="moe_dispatch",
    )(zero_blocks, d0, d1, x, g)


def _cast_rows(src_ref, dst_ref):
    rows = src_ref.shape[0]
    step = _pick(rows, (256, 128))

    def body(j, c):
        r = pl.multiple_of(j * step, step)
        dst_ref[pl.ds(r, step), :] = src_ref[pl.ds(r, step), :].astype(dst_ref.dtype)
        return c

    lax.fori_loop(0, rows // step, body, 0)


def _experts_kernel(be_ref, nu_ref, xb_ref, wg_ref, wu_ref, wd_ref, yb_ref, wgb_ref, wub_ref, wdb_ref):
    i = pl.program_id(0)
    half = xb_ref.shape[1]
    live = i < nu_ref[0]
    fresh = jnp.logical_or(i == 0, be_ref[i] != be_ref[jnp.maximum(i - 1, 0)])

    @pl.when(jnp.logical_and(live, fresh))
    def _():
        _cast_rows(wg_ref, wgb_ref)
        _cast_rows(wu_ref, wub_ref)
        _cast_rows(wd_ref, wdb_ref)

    @pl.when(live)
    def _():
        w = xb_ref[...]
        x_lo = lax.bitcast_convert_type(lax.shift_left(w, jnp.uint32(16)), F32).astype(BF16)
        x_hi = lax.bitcast_convert_type(w & jnp.uint32(0xFFFF0000), F32).astype(BF16)
        h1 = (jnp.dot(x_lo, wgb_ref[:half, :], preferred_element_type=F32)
              + jnp.dot(x_hi, wgb_ref[half:, :], preferred_element_type=F32))
        h2 = (jnp.dot(x_lo, wub_ref[:half, :], preferred_element_type=F32)
              + jnp.dot(x_hi, wub_ref[half:, :], preferred_element_type=F32))
        hid = (h1 * _sigmoid(h1)) * h2
        yb_ref[...] = jnp.dot(hid.astype(BF16), wdb_ref[...], preferred_element_type=F32)

    @pl.when(jnp.logical_not(live))
    def _():
        yb_ref[...] = jnp.zeros_like(yb_ref)


def _experts(blk_exp, n_used, xb, wg, wu, wd, layer):
    p, half = xb.shape
    d = 2 * half
    de = wg.shape[-1]
    n_blk = p // MOE_BLOCK

    def weight(i, be, nu):
        return (layer, be[jnp.minimum(i, nu[0] - 1)], 0, 0)

    return pl.pallas_call(
        _experts_kernel,
        grid_spec=pltpu.PrefetchScalarGridSpec(
            num_scalar_prefetch=2,
            grid=(n_blk,),
            in_specs=[
                pl.BlockSpec((MOE_BLOCK, half), lambda i, be, nu: (jnp.minimum(i, nu[0] - 1), 0)),
                pl.BlockSpec((None, None, d, de), weight),
                pl.BlockSpec((None, None, d, de), weight),
                pl.BlockSpec((None, None, de, d), weight),
            ],
            out_specs=pl.BlockSpec((MOE_BLOCK, d), lambda i, be, nu: (i, 0)),
            scratch_shapes=[pltpu.VMEM((d, de), BF16), pltpu.VMEM((d, de), BF16),
                            pltpu.VMEM((de, d), BF16)],
        ),
        out_shape=jax.ShapeDtypeStruct((p, d), F32),
        compiler_params=_cparams("arbitrary"),
        name="moe_experts",
    )(blk_exp, n_used, xb, wg, wu, wd)


def _combine_kernel(*refs, final):
    if final:
        (d0_ref, d1_ref, d0n_ref, d1n_ref, x_ref, gate_ref, yb_ref, gf_ref,
         o_ref, a_ref, b_ref, sem) = refs
    else:
        d0_ref, d1_ref, d0n_ref, d1n_ref, x_ref, gate_ref, yb_ref, o_ref, a_ref, b_ref, sem = refs
    i = pl.program_id(0)
    last = pl.num_programs(0) - 1
    tm = x_ref.shape[0]
    slot = i % 2

    def gather(s0_ref, s1_ref, s):
        def issue(j, c):
            for u in range(DMA_UNROLL):
                r = j * DMA_UNROLL + u
                pltpu.make_async_copy(yb_ref.at[pl.ds(s0_ref[r], 1)], a_ref.at[s, pl.ds(r, 1)],
                                      sem.at[0, s]).start(priority=u % 2)
                pltpu.make_async_copy(yb_ref.at[pl.ds(s1_ref[r], 1)], b_ref.at[s, pl.ds(r, 1)],
                                      sem.at[1, s]).start(priority=(u + 1) % 2)
            return c

        lax.fori_loop(0, tm // DMA_UNROLL, issue, 0)

    @pl.when(i == 0)
    def _():
        gather(d0_ref, d1_ref, 0)

    @pl.when(i < last)
    def _():
        gather(d0n_ref, d1n_ref, 1 - slot)

    pltpu.make_async_copy(yb_ref.at[pl.ds(0, tm)], a_ref.at[slot], sem.at[0, slot]).wait()
    pltpu.make_async_copy(yb_ref.at[pl.ds(0, tm)], b_ref.at[slot], sem.at[1, slot]).wait()

    def body(j, c):
        r = pl.multiple_of(j * ROW_CHUNK, ROW_CHUNK)
        rs = pl.ds(r, ROW_CHUNK)
        gates = gate_ref[rs, :]
        y = gates[:, 0:1] * a_ref[slot, rs, :] + gates[:, 1:2] * b_ref[slot, rs, :]
        xn = x_ref[rs, :] + y
        if final:
            ms = jnp.mean(xn * xn, axis=-1, keepdims=True)
            xn = xn * lax.rsqrt(ms + EPS) * gf_ref[...]
        o_ref[rs, :] = xn
        return c

    lax.fori_loop(0, tm // ROW_CHUNK, body, 0)


def _combine(d0, d1, x, gates, yb, g_final):
    t, d = x.shape
    tm = _pick(t, (256, 128))
    n = t // tm
    final = g_final is not None

    def nxt(i):
        return (jnp.minimum(i + 1, n - 1),)

    in_specs = [
        pl.BlockSpec((tm,), lambda i: (i,), memory_space=pltpu.SMEM),
        pl.BlockSpec((tm,), lambda i: (i,), memory_space=pltpu.SMEM),
        pl.BlockSpec((tm,), nxt, memory_space=pltpu.SMEM),
        pl.BlockSpec((tm,), nxt, memory_space=pltpu.SMEM),
        pl.BlockSpec((tm, d), lambda i: (i, 0)),
        pl.BlockSpec((tm, LANES), lambda i: (i, 0)),
        pl.BlockSpec(memory_space=pl.ANY),
    ]
    args = [d0, d1, d0, d1, x, gates, yb]
    if final:
        in_specs.append(pl.BlockSpec((1, d), lambda i: (0, 0)))
        args.append(g_final)
    return pl.pallas_call(
        functools.partial(_combine_kernel, final=final),
        grid=(n,),
        in_specs=in_specs,
        out_specs=pl.BlockSpec((tm, d), lambda i: (i, 0)),
        out_shape=jax.ShapeDtypeStruct((t, d), F32),
        scratch_shapes=[pltpu.VMEM((2, tm, d), F32), pltpu.VMEM((2, tm, d), F32),
                        pltpu.SemaphoreType.DMA((2, 2))],
        compiler_params=_cparams("arbitrary", disable_bounds_checks=True),
        name="moe_combine",
    )(*args)


def _moe(x, g_norm, w_route, b_route, wg, wu, wd, layer, g_final, *, n_grp, epg):
    t, d = x.shape
    n_exp = n_grp * epg
    eid, rank, gates, cnt = _router(x, g_norm, w_route, b_route, n_grp=n_grp, epg=epg)

    counts = cnt[n_grp:n_grp + n_exp, 0].astype(I32)
    pblocks = (counts + MOE_BLOCK - 1) // MOE_BLOCK
    bends = jnp.cumsum(pblocks)
    pstarts = (bends - pblocks) * MOE_BLOCK
    n_blk = (t * TOP_K) // MOE_BLOCK + n_exp
    blk = jnp.arange(n_blk, dtype=I32)
    blk_exp = jnp.minimum(jnp.sum(bends[None, :] <= blk[:, None], axis=1), n_exp - 1).astype(I32)
    n_used = bends[-1:].astype(I32)
    last_blk = jnp.where(counts > 0, bends - 1, -1)
    trail = n_used + jnp.arange(n_exp, dtype=I32)
    zero_blocks = jnp.concatenate([last_blk, jnp.where(trail < n_blk, trail, -1)]).astype(I32)

    slots = _slots(pstarts.astype(I32), eid, rank)
    d0, d1 = slots[0], slots[1]
    xb = _dispatch(zero_blocks, d0, d1, x, g_norm, n_blk * MOE_BLOCK)
    yb = _experts(blk_exp, n_used, xb, wg, wu, wd, layer)
    return _combine(d0, d1, x, gates, yb, g_final)


def _row(v):
    return v.reshape(1, -1).astype(F32)


def _trunk(x, p):
    b, s, d = x.shape
    t = b * s
    depth = p["norm_mix"].shape[0]
    xt = x.reshape(t, d)
    for i in range(depth):
        j = i // 2
        if i % 2 == 0:
            z, kt, gates = _inproj_mlstm(xt, _row(p["norm_mix"][i]), p["mlstm_w_main"][j],
                                         p["mlstm_w_kt"][j], p["mlstm_w_gate"][j])
            ng = 4 * MLSTM_HEADS
            gcol = gates.reshape(b, s, N_GATE_LANES)
            grow = jnp.swapaxes(gcol[:, :, :ng], 1, 2)
            brow, bcol = p["mlstm_b_row"][j], p["mlstm_b_col"][j]
            z3 = z.reshape(b, s, -1)
            hf = _mlstm(z3, kt, gcol, grow, brow, bcol, reverse=False)
            hn = _mlstm(z3, kt, gcol, grow, brow, bcol, reverse=True,
                        fused=(hf, _row(p["mlstm_g_hnorm"][j])))
            xt = _outproj(hn.reshape(t, -1), p["mlstm_w_out"][j], xt)
        else:
            z, ssq = _inproj_sgu(xt, _row(p["norm_mix"][i]), p["sgu_w_in"][j])
            xt = _sgu_mix(z, ssq, _row(p["sgu_g_v"][j]), p["sgu_w_s"][j], p["sgu_b_s"][j],
                          p["sgu_w_out"][j], xt)
        g_final = _row(p["norm_final"]) if i == depth - 1 else None
        xt = _moe(xt, _row(p["norm_ffn"][i]), p["moe_w_route"][i], p["moe_b_route"][i],
                  p["moe_w_gate"], p["moe_w_up"], p["moe_w_down"], i, g_final,
                  n_grp=p["n_grp"], epg=p["epg"])
    if depth == 0:
        raise ValueError("depth 0 is not supported")
    return xt.reshape(b, s, d)


def kernel(x_prompt, x_sample, norm_mix, norm_ffn, norm_final, mlstm_w_in, mlstm_b_gates, mlstm_g_hnorm, mlstm_w_out, sgu_w_in, sgu_g_v, sgu_w_s, sgu_b_s, sgu_w_out, moe_w_grp, moe_b_grp, moe_w_exp, moe_b_exp, moe_w_gate, moe_w_up, moe_w_down):
    ng = 4 * MLSTM_HEADS
    n_main = mlstm_w_in.shape[-1] - ng
    qk_dim = (n_main - 2 * mlstm_w_out.shape[-2]) // 2
    n_grp = moe_w_grp.shape[-1]
    n_exp = moe_w_exp.shape[-1]
    assert n_grp + n_exp <= LANES and ng <= N_GATE_LANES

    def pad_lanes(a, width):
        return jnp.pad(a, [(0, 0)] * (a.ndim - 1) + [(0, width - a.shape[-1])])

    p = dict(
        norm_mix=norm_mix, norm_ffn=norm_ffn, norm_final=norm_final,
        mlstm_w_main=jnp.concatenate([mlstm_w_in[..., 2 * qk_dim:n_main], mlstm_w_in[..., :qk_dim]],
                                     axis=-1).astype(BF16),
        mlstm_w_kt=jnp.swapaxes(mlstm_w_in[..., qk_dim:2 * qk_dim], -1, -2).astype(BF16),
        mlstm_w_gate=pad_lanes(mlstm_w_in[..., n_main:], N_GATE_LANES).astype(BF16),
        mlstm_b_row=pad_lanes(mlstm_b_gates, N_GATE_LANES)[:, None, :].astype(F32),
        mlstm_b_col=jnp.broadcast_to(mlstm_b_gates[:, :, None], mlstm_b_gates.shape + (MLSTM_CHUNK,)).astype(F32),
        mlstm_g_hnorm=mlstm_g_hnorm,
        mlstm_w_out=mlstm_w_out.astype(BF16),
        sgu_w_in=sgu_w_in.astype(BF16),
        sgu_g_v=sgu_g_v,
        sgu_w_s=sgu_w_s.astype(BF16),
        sgu_b_s=jnp.broadcast_to(sgu_b_s[..., None], sgu_b_s.shape + (LANES,)).astype(F32),
        sgu_w_out=sgu_w_out.astype(BF16),
        moe_w_route=pad_lanes(jnp.concatenate([moe_w_grp, moe_w_exp], axis=-1), LANES).astype(BF16),
        moe_b_route=pad_lanes(jnp.concatenate([moe_b_grp, moe_b_exp], axis=-1), LANES)[:, None, :].astype(F32),
        moe_w_gate=moe_w_gate, moe_w_up=moe_w_up, moe_w_down=moe_w_down,
        n_grp=n_grp, epg=n_exp // n_grp,
    )
    return (_trunk(x_prompt, p), _trunk(x_sample, p))
```

```python
import functools

import jax
import jax.numpy as jnp
import numpy as np
from jax import lax
from jax.experimental import pallas as pl
from jax.experimental.pallas import tpu as pltpu

F32 = jnp.float32
BF16 = jnp.bfloat16
U32 = jnp.uint32
I32 = jnp.int32

EPS = 1e-6
MLSTM_HEADS = 4
MLSTM_CHUNK = 128
GATE_SOFTCAP = 15.0
N_GATE_LANES = 128
SGU_CHUNK = 128
TOP_K = 2
ROW_CHUNK = 128
MOE_BLOCK = 512
DMA_UNROLL = 8
LANES = 128
VMEM_LIMIT_BYTES = 56 * 1024 * 1024


def _cparams(*sem, **kw):
    return pltpu.CompilerParams(dimension_semantics=sem, vmem_limit_bytes=VMEM_LIMIT_BYTES, **kw)


def _pick(n, prefs):
    for p in prefs:
        if n % p == 0:
            return p
    return n


def _rmsnorm_rows(x_ref, g_ref, out_ref):
    rows = x_ref.shape[0]
    g = g_ref[...]

    def body(i, c):
        r = pl.multiple_of(i * ROW_CHUNK, ROW_CHUNK)
        x = x_ref[pl.ds(r, ROW_CHUNK), :]
        ms = jnp.mean(x * x, axis=-1, keepdims=True)
        out_ref[pl.ds(r, ROW_CHUNK), :] = (x * lax.rsqrt(ms + EPS) * g).astype(out_ref.dtype)
        return c

    lax.fori_loop(0, rows // ROW_CHUNK, body, 0)


def _sigmoid(x):
    return 1.0 / (1.0 + jnp.exp(-x))


def _log_sigmoid(x):
    return jnp.minimum(x, 0.0) - jnp.log1p(jnp.exp(-jnp.abs(x)))


def _soft_cap(t):
    return GATE_SOFTCAP * jnp.tanh(t / GATE_SOFTCAP)


def _split3(a):
    hi = a.astype(BF16)
    r1 = a - hi.astype(F32)
    mid = r1.astype(BF16)
    lo = (r1 - mid.astype(F32)).astype(BF16)
    return [hi, mid, lo]


def _resident(shape):
    return pl.BlockSpec(shape, lambda i: (0,) * len(shape), pipeline_mode=pl.Buffered(1))


def _col_chunks(n):
    step = _pick(n, (1024, 512, 256, 128))
    return [(c, step) for c in range(0, n, step)]


def _inproj_mlstm_kernel(x_ref, g_ref, w_ref, wkt_ref, wg_ref, z_ref, kt_ref, gate_ref, xn_ref):
    _rmsnorm_rows(x_ref, g_ref, xn_ref)
    xn = xn_ref[...]
    gate_ref[...] = jnp.dot(xn, wg_ref[...], preferred_element_type=F32)
    kt_ref[...] = lax.dot_general(wkt_ref[...], xn, (((1,), (1,)), ((), ())),
                                  preferred_element_type=F32).astype(kt_ref.dtype)
    for c, w in _col_chunks(w_ref.shape[1]):
        z_ref[:, c:c + w] = jnp.dot(xn, w_ref[:, c:c + w], preferred_element_type=F32).astype(z_ref.dtype)


def _inproj_mlstm(x, g, w, wkt, wg):
    t, d = x.shape
    n = w.shape[1]
    qk = wkt.shape[0]
    tm = _pick(t, (512, 256, 128))
    return pl.pallas_call(
        _inproj_mlstm_kernel,
        grid=(t // tm,),
        in_specs=[
            pl.BlockSpec((tm, d), lambda i: (i, 0)),
            _resident((1, d)),
            _resident((d, n)),
            _resident((qk, d)),
            _resident((d, N_GATE_LANES)),
        ],
        out_specs=[
            pl.BlockSpec((tm, n), lambda i: (i, 0)),
            pl.BlockSpec((qk, tm), lambda i: (0, i)),
            pl.BlockSpec((tm, N_GATE_LANES), lambda i: (i, 0)),
        ],
        out_shape=[
            jax.ShapeDtypeStruct((t, n), BF16),
            jax.ShapeDtypeStruct((qk, t), BF16),
            jax.ShapeDtypeStruct((t, N_GATE_LANES), F32),
        ],
        scratch_shapes=[pltpu.VMEM((tm, d), BF16)],
        compiler_params=_cparams("parallel"),
        name="inproj_mlstm",
    )(x, g, w, wkt, wg)


def _mlstm_kernel(*refs, reverse, fuse, heads, dk, dv):
    n_in = 11 if fuse else 8
    if fuse:
        (q_ref, kt_ref, v_ref, gcol_ref, grow_ref, brow_ref, bcol_ref, sel_ref,
         o_ref, hf_ref, gh_ref) = refs[:n_in]
    else:
        q_ref, kt_ref, v_ref, gcol_ref, grow_ref, brow_ref, bcol_ref, sel_ref = refs[:n_in]
    out_ref = refs[n_in]
    state = refs[n_in + 1:]
    c_refs, m_refs = state[0::2], state[1::2]
    L = MLSTM_CHUNK

    @pl.when(pl.program_id(1) == 0)
    def _():
        for ref in state:
            ref[...] = jnp.zeros_like(ref)

    gc = _soft_cap(gcol_ref[...] + brow_ref[...])
    gr = _soft_cap(grow_ref[...] + bcol_ref[...])
    lsc = _log_sigmoid(gc)
    lsr = _log_sigmoid(gr)
    row = lax.broadcasted_iota(I32, (L, L), 0)
    col = lax.broadcasted_iota(I32, (L, L), 1)
    lower = col <= row
    upper = col >= row
    mask_t, mask_s = (upper, lower) if reverse else (lower, upper)
    ng = gr.shape[0]
    incl_t = jnp.where(mask_t, 1.0, 0.0).astype(BF16)
    incl_s = jnp.where(mask_s, 1.0, 0.0).astype(BF16)
    cc = jnp.dot(incl_t, jnp.concatenate(_split3(lsc), axis=1), preferred_element_type=F32)
    cum_cols = cc[:, :LANES] + cc[:, LANES:2 * LANES] + cc[:, 2 * LANES:]
    cr = jnp.dot(jnp.concatenate(_split3(lsr), axis=0), incl_s, preferred_element_type=F32)
    cum_rows = cr[:ng] + cr[ng:2 * ng] + cr[2 * ng:]
    spread = jnp.dot(jnp.concatenate(_split3(gc) + _split3(cum_cols), axis=1), sel_ref[...],
                     preferred_element_type=F32)
    end = 0 if reverse else L - 1
    scale = dk ** -0.5
    reps = dv // LANES

    def lanes(a, n):
        return jnp.concatenate([a] * n, axis=1) if n > 1 else a

    stage1 = []
    for h in range(heads):
        ci = (2 * heads if reverse else 0) + h
        cf = ci + heads
        i_col = spread[:, 2 * h * LANES:(2 * h + 1) * LANES]
        b_col = spread[:, (2 * h + 1) * LANES:(2 * h + 2) * LANES]
        i_row = gr[ci:ci + 1, :]
        b_row = cum_rows[cf:cf + 1, :]
        g = b_col[end:end + 1, :]
        m_prev = m_refs[h][...]
        a_col = g - b_col + i_col
        m_new = jnp.maximum(g + m_prev, jnp.max(a_col, axis=0, keepdims=True))
        decay = jnp.exp(g + m_prev - m_new)
        w_row = jnp.exp(g - b_row + i_row - m_new)

        q = q_ref[:, h * dk:(h + 1) * dk]
        kt = kt_ref[h * dk:(h + 1) * dk, :]
        qs_b = (q.astype(F32) * scale).astype(BF16)
        cx_old = c_refs[h][...]
        num_x = jnp.dot(qs_b, cx_old.astype(BF16), preferred_element_type=F32)
        qk = jnp.dot(qs_b, kt, preferred_element_type=F32)

        log_inter = b_col + m_prev
        dlog = jnp.where(mask_t, b_col - b_row + i_row, -jnp.inf)
        m_t = jnp.maximum(log_inter, jnp.max(dlog, axis=1, keepdims=True))
        stage1.append((m_new, decay, w_row, kt, cx_old, num_x, qk, log_inter, dlog, m_t))

    ones = jnp.ones((L, LANES), BF16)
    for h in range(heads):
        m_new, decay, w_row, kt, cx_old, num_x, qk, log_inter, dlog, m_t = stage1[h]
        vx = jnp.concatenate([v_ref[:, h * dv:(h + 1) * dv], ones], axis=1)
        s = qk * jnp.exp(dlog - m_t)
        inter = jnp.exp(log_inter - m_t)
        sv = jnp.dot(s.astype(BF16), vx, preferred_element_type=F32)
        num = lanes(inter, reps) * num_x[:, :dv] + sv[:, :dv]
        den = inter * num_x[:, dv:] + sv[:, dv:]
        hh = num * lanes(1.0 / jnp.maximum(jnp.abs(den), jnp.exp(-m_t)), reps)

        kwt = (kt.astype(F32) * w_row).astype(BF16)
        c_refs[h][...] = lanes(decay, reps + 1) * cx_old + jnp.dot(kwt, vx, preferred_element_type=F32)
        m_refs[h][...] = m_new

        hs = slice(h * dv, (h + 1) * dv)
        if fuse:
            tot = hf_ref[:, hs] + hh
            ms = jnp.mean(tot * tot, axis=-1, keepdims=True)
            gated = tot * lax.rsqrt(ms + EPS) * gh_ref[:, hs] * _sigmoid(o_ref[:, hs].astype(F32))
            out_ref[:, hs] = gated.astype(out_ref.dtype)
        else:
            out_ref[:, hs] = hh


def _mlstm(z, kt, gcol, grow, brow, bcol, *, reverse, fused=None):
    b, s, width = z.shape
    heads = MLSTM_HEADS
    qk = kt.shape[0]
    vdim = (width - qk) // 2
    dk, dv = qk // heads, vdim // heads
    nc = s // MLSTM_CHUNK
    L = MLSTM_CHUNK
    ng = grow.shape[1]
    assert L == N_GATE_LANES and dv % LANES == 0 and (2 * vdim) % qk == 0

    def ch(c):
        return nc - 1 - c if reverse else c

    in_specs = [
        pl.BlockSpec((None, L, qk), lambda bi, c: (bi, ch(c), (2 * vdim) // qk)),
        pl.BlockSpec((qk, L), lambda bi, c: (0, bi * nc + ch(c))),
        pl.BlockSpec((None, L, vdim), lambda bi, c: (bi, ch(c), 0)),
        pl.BlockSpec((None, L, N_GATE_LANES), lambda bi, c: (bi, ch(c), 0)),
        pl.BlockSpec((None, ng, L), lambda bi, c: (bi, 0, ch(c))),
        pl.BlockSpec((1, N_GATE_LANES), lambda bi, c: (0, 0)),
        pl.BlockSpec((ng, L), lambda bi, c: (0, 0)),
        pl.BlockSpec((6 * LANES, 2 * heads * LANES), lambda bi, c: (0, 0)),
    ]
    sel = np.zeros((6, LANES, 2 * heads, LANES), np.float32)
    for h in range(heads):
        ci = (2 * heads if reverse else 0) + h
        sel[0:3, ci, 2 * h, :] = 1.0
        sel[3:6, ci + heads, 2 * h + 1, :] = 1.0
    sel = jnp.asarray(sel.reshape(6 * LANES, 2 * heads * LANES), BF16)
    args = [z, kt, z, gcol, grow, brow, bcol, sel]
    if fused is not None:
        hf, gh = fused
        in_specs += [
            pl.BlockSpec((None, L, vdim), lambda bi, c: (bi, ch(c), 1)),
            pl.BlockSpec((None, L, vdim), lambda bi, c: (bi, ch(c), 0)),
            pl.BlockSpec((1, vdim), lambda bi, c: (0, 0)),
        ]
        args += [z, hf, gh]
    return pl.pallas_call(
        functools.partial(_mlstm_kernel, reverse=reverse, fuse=fused is not None,
                          heads=heads, dk=dk, dv=dv),
        grid=(b, nc),
        in_specs=in_specs,
        out_specs=pl.BlockSpec((None, L, vdim), lambda bi, c: (bi, ch(c), 0)),
        out_shape=jax.ShapeDtypeStruct((b, s, vdim), BF16 if fused is not None else F32),
        scratch_shapes=[pltpu.VMEM((dk, dv + LANES), F32), pltpu.VMEM((1, LANES), F32)] * heads,
        compiler_params=_cparams("parallel", "arbitrary"),
        name="mlstm_bwd" if reverse else "mlstm_fwd",
    )(*args)


def _outproj_kernel(h_ref, w_ref, x_ref, o_ref):
    h = h_ref[...]
    for c, w in _col_chunks(w_ref.shape[1]):
        o_ref[:, c:c + w] = x_ref[:, c:c + w] + jnp.dot(h, w_ref[:, c:c + w], preferred_element_type=F32)


def _outproj(h, w, x):
    t, kdim = h.shape
    n = w.shape[1]
    tm = _pick(t, (512, 256, 128))
    return pl.pallas_call(
        _outproj_kernel,
        grid=(t // tm,),
        in_specs=[
            pl.BlockSpec((tm, kdim), lambda i: (i, 0)),
            _resident((kdim, n)),
            pl.BlockSpec((tm, n), lambda i: (i, 0)),
        ],
        out_specs=pl.BlockSpec((tm, n), lambda i: (i, 0)),
        out_shape=jax.ShapeDtypeStruct((t, n), F32),
        compiler_params=_cparams("parallel"),
        name="outproj",
    )(h, w, x)


def _inproj_sgu_kernel(x_ref, g_ref, w_ref, z_ref, ssq_ref, xn_ref):
    _rmsnorm_rows(x_ref, g_ref, xn_ref)
    xn = xn_ref[...]
    n = w_ref.shape[1]
    part = None
    for c, w in _col_chunks(n // 2) + [(n // 2 + c, w) for c, w in _col_chunks(n // 2)]:
        z = jax.nn.gelu(jnp.dot(xn, w_ref[:, c:c + w], preferred_element_type=F32))
        z_ref[:, c:c + w] = z.astype(z_ref.dtype)
        if c >= n // 2:
            z2 = z * z
            for l in range(0, w, LANES):
                part = z2[:, l:l + LANES] if part is None else part + z2[:, l:l + LANES]
    ssq_ref[...] = part


def _inproj_sgu(x, g, w):
    t, d = x.shape
    n = w.shape[1]
    tm = _pick(t, (512, 256, 128))
    return pl.pallas_call(
        _inproj_sgu_kernel,
        grid=(t // tm,),
        in_specs=[
            pl.BlockSpec((tm, d), lambda i: (i, 0)),
            _resident((1, d)),
            _resident((d, n)),
        ],
        out_specs=[
            pl.BlockSpec((tm, n), lambda i: (i, 0)),
            pl.BlockSpec((tm, LANES), lambda i: (i, 0)),
        ],
        out_shape=[
            jax.ShapeDtypeStruct((t, n), BF16),
            jax.ShapeDtypeStruct((t, LANES), F32),
        ],
        scratch_shapes=[pltpu.VMEM((tm, d), BF16)],
        compiler_params=_cparams("parallel"),
        name="inproj_sgu",
    )(x, g, w)


def _sgu_mix_kernel(u_ref, v_ref, ssq_ref, gv_ref, ws_ref, bs_ref, wo_ref, x_ref, o_ref, gated_ref,
                    *, groups, gd):
    rows = u_ref.shape[0]
    width = v_ref.shape[1]
    gv = gv_ref[...]
    for c in range(rows // SGU_CHUNK):
        rs = slice(c * SGU_CHUNK, (c + 1) * SGU_CHUNK)
        ms = jnp.sum(ssq_ref[rs, :], axis=-1, keepdims=True) / width
        vn = (v_ref[rs, :].astype(F32) * lax.rsqrt(ms + EPS) * gv).astype(BF16)
        for gi in range(groups):
            cs = slice(gi * gd, (gi + 1) * gd)
            mixed = jnp.dot(ws_ref[gi], vn[:, cs], preferred_element_type=F32)
            bias = bs_ref[gi]
            bias = jnp.concatenate([bias] * (gd // LANES), axis=1) if gd > LANES else bias
            gated_ref[rs, cs] = (u_ref[rs, cs].astype(F32) * (mixed + bias)).astype(BF16)
    gated = gated_ref[...]
    for c, w in _col_chunks(wo_ref.shape[1]):
        o_ref[:, c:c + w] = x_ref[:, c:c + w] + jnp.dot(gated, wo_ref[:, c:c + w], preferred_element_type=F32)


def _sgu_mix(z, ssq, gv, ws, bs, wo, x):
    t, d = x.shape
    width = z.shape[1] // 2
    groups = ws.shape[0]
    tm = _pick(t, (512, 256, 128))
    return pl.pallas_call(
        functools.partial(_sgu_mix_kernel, groups=groups, gd=width // groups),
        grid=(t // tm,),
        in_specs=[
            pl.BlockSpec((tm, width), lambda i: (i, 0)),
            pl.BlockSpec((tm, width), lambda i: (i, 1)),
            pl.BlockSpec((tm, LANES), lambda i: (i, 0)),
            _resident((1, width)),
            _resident((groups, SGU_CHUNK, SGU_CHUNK)),
            _resident((groups, SGU_CHUNK, LANES)),
            _resident((width, d)),
            pl.BlockSpec((tm, d), lambda i: (i, 0)),
        ],
        out_specs=pl.BlockSpec((tm, d), lambda i: (i, 0)),
        out_shape=jax.ShapeDtypeStruct((t, d), F32),
        scratch_shapes=[pltpu.VMEM((tm, width), BF16)],
        compiler_params=_cparams("parallel"),
        name="sgu_mix",
    )(z, z, ssq, gv, ws, bs, wo, x)


def _router_kernel(x_ref, g_ref, w_ref, b_ref, eid_ref, rank_ref, gate_ref, cnt_ref, hn_ref, carry_ref,
                   *, n_grp, epg, rows):
    tm = x_ref.shape[0]

    @pl.when(pl.program_id(0) == 0)
    def _():
        carry_ref[...] = jnp.zeros_like(carry_ref)

    _rmsnorm_rows(x_ref, g_ref, hn_ref)
    logits = jnp.dot(hn_ref[...], w_ref[...], preferred_element_type=F32) + b_ref[...]
    lt = logits.T[:rows, :]
    ridx = lax.broadcasted_iota(I32, (rows, tm), 0).astype(F32)
    big = float(rows)

    gl = jnp.where(ridx < n_grp, lt, -jnp.inf)
    gmax = jnp.max(gl, axis=0, keepdims=True)
    grp = jnp.min(jnp.where(gl == gmax, ridx, big), axis=0, keepdims=True)
    grp_p = 1.0 / jnp.sum(jnp.exp(gl - gmax), axis=0, keepdims=True)

    lo = n_grp + grp * epg
    el = jnp.where(ridx >= lo, jnp.where(ridx < lo + epg, lt, -jnp.inf), -jnp.inf)
    emax = jnp.max(el, axis=0, keepdims=True)
    ee = jnp.exp(el - emax)
    p = ee / jnp.sum(ee, axis=0, keepdims=True)
    pm = jnp.where(ridx >= lo, jnp.where(ridx < lo + epg, p, -1.0), -1.0)
    p1 = jnp.max(pm, axis=0, keepdims=True)
    i1 = jnp.min(jnp.where(pm == p1, ridx, big), axis=0, keepdims=True)
    pm2 = jnp.where(ridx == i1, -1.0, pm)
    p2 = jnp.max(pm2, axis=0, keepdims=True)
    i2 = jnp.min(jnp.where(pm2 == p2, ridx, big), axis=0, keepdims=True)
    psum = p1 + p2
    g1 = grp_p * (p1 / psum)
    g2 = grp_p * (p2 / psum)

    hit1 = ridx == i1
    hit2 = ridx == i2
    onehot = jnp.where(hit1, 1.0, jnp.where(hit2, 1.0, 0.0))
    r = lax.broadcasted_iota(I32, (tm, tm), 0)
    c = lax.broadcasted_iota(I32, (tm, tm), 1)
    earlier = jnp.where(r < c, 1.0, 0.0).astype(BF16)
    carry = carry_ref[:, 0:1]
    before = jnp.dot(onehot.astype(BF16), earlier, preferred_element_type=F32) + carry
    rank1 = jnp.sum(jnp.where(hit1, before, 0.0), axis=0, keepdims=True)
    rank2 = jnp.sum(jnp.where(hit2, before, 0.0), axis=0, keepdims=True)
    carry = carry + jnp.sum(onehot, axis=1, keepdims=True)
    carry_ref[...] = jnp.broadcast_to(carry, carry_ref.shape)
    cnt_ref[...] = jnp.broadcast_to(carry, cnt_ref.shape)

    sub = lax.broadcasted_iota(I32, (8, tm), 0)
    eid_ref[...] = jnp.where(sub == 0, i1 - n_grp, jnp.where(sub == 1, i2 - n_grp, 0.0)).astype(I32)
    rank_ref[...] = jnp.where(sub == 0, rank1, jnp.where(sub == 1, rank2, 0.0)).astype(I32)
    sub_l = lax.broadcasted_iota(I32, (LANES, tm), 0)
    gate_ref[...] = jnp.where(sub_l == 0, g1, jnp.where(sub_l == 1, g2, 0.0)).T


def _router(x, g, w, b, *, n_grp, epg):
    t, d = x.shape
    tm = _pick(t, (512, 256, 128))
    rows = -(-(n_grp + n_grp * epg) // 8) * 8
    return pl.pallas_call(
        functools.partial(_router_kernel, n_grp=n_grp, epg=epg, rows=rows),
        grid=(t // tm,),
        in_specs=[
            pl.BlockSpec((tm, d), lambda i: (i, 0)),
            pl.BlockSpec((1, d), lambda i: (0, 0)),
            pl.BlockSpec((d, LANES), lambda i: (0, 0)),
            pl.BlockSpec((1, LANES), lambda i: (0, 0)),
        ],
        out_specs=[
            pl.BlockSpec((8, tm), lambda i: (0, i)),
            pl.BlockSpec((8, tm), lambda i: (0, i)),
            pl.BlockSpec((tm, LANES), lambda i: (i, 0)),
            pl.BlockSpec((rows, LANES), lambda i: (0, 0)),
        ],
        out_shape=[
            jax.ShapeDtypeStruct((8, t), I32),
            jax.ShapeDtypeStruct((8, t), I32),
            jax.ShapeDtypeStruct((t, LANES), F32),
            jax.ShapeDtypeStruct((rows, LANES), F32),
        ],
        scratch_shapes=[pltpu.VMEM((tm, d), BF16), pltpu.VMEM((rows, LANES), F32)],
        compiler_params=_cparams("arbitrary"),
        name="router",
    )(x, g, w, b)


def _slots_kernel(ps_ref, eid_ref, rank_ref, slot_ref, *, n_exp):
    eid = eid_ref[...]
    base = jnp.zeros_like(eid)
    for e in range(n_exp):
        base = jnp.where(eid == e, ps_ref[e], base)
    slot_ref[...] = base + rank_ref[...]


def _slots(pstarts, eid, rank):
    rows, t = eid.shape
    tm = _pick(t, (4096, 2048, 1024, 512, 256, 128))
    return pl.pallas_call(
        functools.partial(_slots_kernel, n_exp=pstarts.shape[0]),
        grid_spec=pltpu.PrefetchScalarGridSpec(
            num_scalar_prefetch=1,
            grid=(t // tm,),
            in_specs=[
                pl.BlockSpec((rows, tm), lambda i, ps: (0, i)),
                pl.BlockSpec((rows, tm), lambda i, ps: (0, i)),
            ],
            out_specs=pl.BlockSpec((rows, tm), lambda i, ps: (0, i)),
        ),
        out_shape=jax.ShapeDtypeStruct((rows, t), I32),
        compiler_params=_cparams("parallel"),
        name="moe_slots",
    )(pstarts, eid, rank)


def _dispatch_kernel(zb_ref, d0_ref, d1_ref, x_ref, g_ref, xb_ref, pk_ref, zero_ref, sem, zsem):
    i = pl.program_id(0)
    last = pl.num_programs(0) - 1
    tm, d = x_ref.shape
    half = d // 2
    rt = half // LANES
    slot = i % 2
    g = g_ref[...]

    @pl.when(i == 0)
    def _():
        zero_ref[...] = jnp.zeros_like(zero_ref)

        def zero_copy(j):
            lines = MOE_BLOCK * rt
            start = pl.multiple_of(jnp.maximum(zb_ref[j], 0) * lines, lines)
            return pltpu.make_async_copy(zero_ref, xb_ref.at[pl.ds(start, lines)], zsem.at[0])

        def zstart(j, c):
            @pl.when(zb_ref[j] >= 0)
            def _():
                zero_copy(j).start()
            return c

        def zwait(j, c):
            @pl.when(zb_ref[j] >= 0)
            def _():
                zero_copy(j).wait()
            return c

        lax.fori_loop(0, zb_ref.shape[0], zstart, 0)
        lax.fori_loop(0, zb_ref.shape[0], zwait, 0)

    def pack(j, c):
        r = pl.multiple_of(j * ROW_CHUNK, ROW_CHUNK)
        x = x_ref[pl.ds(r, ROW_CHUNK), :]
        ms = jnp.mean(x * x, axis=-1, keepdims=True)
        hn = (x * lax.rsqrt(ms + EPS) * g).astype(BF16).astype(F32)
        bits = lax.bitcast_convert_type(hn, U32)
        lo = lax.shift_right_logical(bits[:, :half], jnp.uint32(16))
        hi = bits[:, half:] & jnp.uint32(0xFFFF0000)
        words = hi | lo
        for s in range(rt):
            pk_ref[slot, pl.ds(r * rt + s, ROW_CHUNK, stride=rt), :] = words[:, s * LANES:(s + 1) * LANES]
        return c

    lax.fori_loop(0, tm // ROW_CHUNK, pack, 0)

    def row_copy(r, dest, prio):
        src = pk_ref.at[slot, pl.ds(pl.multiple_of(r * rt, rt), rt)]
        dst = xb_ref.at[pl.ds(pl.multiple_of(dest * rt, rt), rt)]
        pltpu.make_async_copy(src, dst, sem.at[slot]).start(priority=prio)

    def issue(j, c):
        for u in range(DMA_UNROLL):
            r = j * DMA_UNROLL + u
            row_copy(r, d0_ref[r], u % 2)
            row_copy(r, d1_ref[r], (u + 1) % 2)
        return c

    lax.fori_loop(0, tm // DMA_UNROLL, issue, 0)

    def drain(s):
        for _ in range(TOP_K):
            pltpu.make_async_copy(pk_ref.at[s], xb_ref.at[pl.ds(0, tm * rt)], sem.at[s]).wait()

    @pl.when(i > 0)
    def _():
        drain(1 - slot)

    @pl.when(i == last)
    def _():
        drain(slot)


def _dispatch(zero_blocks, d0, d1, x, g, n_rows):
    t, d = x.shape
    tm = _pick(t, (256, 128))
    rt = d // 2 // LANES
    return pl.pallas_call(
        _dispatch_kernel,
        grid_spec=pltpu.PrefetchScalarGridSpec(
            num_scalar_prefetch=1,
            grid=(t // tm,),
            in_specs=[
                pl.BlockSpec((tm,), lambda i, zb: (i,), memory_space=pltpu.SMEM),
                pl.BlockSpec((tm,), lambda i, zb: (i,), memory_space=pltpu.SMEM),
                pl.BlockSpec((tm, d), lambda i, zb: (i, 0)),
                pl.BlockSpec((1, d), lambda i, zb: (0, 0)),
            ],
            out_specs=pl.BlockSpec(memory_space=pl.ANY),
            scratch_shapes=[pltpu.VMEM((2, tm * rt, LANES), U32), pltpu.VMEM((MOE_BLOCK * rt, LANES), U32),
                            pltpu.SemaphoreType.DMA((2,)), pltpu.SemaphoreType.DMA((1,))],
        ),
        out_shape=jax.ShapeDtypeStruct((n_rows * rt, LANES), U32),
        compiler_params=_cparams("arbitrary", disable_bounds_checks=True),
        name="moe_dispatch",
    )(zero_blocks, d0, d1, x, g)


def _cast_rows(src_ref, dst_ref):
    rows = src_ref.shape[0]
    step = _pick(rows, (256, 128))

    def body(j, c):
        r = pl.multiple_of(j * step, step)
        dst_ref[pl.ds(r, step), :] = src_ref[pl.ds(r, step), :].astype(dst_ref.dtype)
        return c

    lax.fori_loop(0, rows // step, body, 0)


def _experts_kernel(be_ref, nu_ref, xb_ref, wg_ref, wu_ref, wd_ref, yb_ref, wgb_ref, wub_ref, wdb_ref):
    s = pl.program_id(0)
    nu = nu_ref[0]
    blk = s - 1
    half = wgb_ref.shape[0] // 2
    rt_in = half // LANES

    @pl.when(jnp.logical_and(s >= 1, blk < nu))
    def _():
        w = jnp.concatenate([xb_ref[pl.ds(t, MOE_BLOCK, stride=rt_in), :] for t in range(rt_in)], axis=1)
        x_lo = lax.bitcast_convert_type(lax.shift_left(w, jnp.uint32(16)), F32).astype(BF16)
        x_hi = lax.bitcast_convert_type(w & jnp.uint32(0xFFFF0000), F32).astype(BF16)
        h1 = (jnp.dot(x_lo, wgb_ref[:half, :], preferred_element_type=F32)
              + jnp.dot(x_hi, wgb_ref[half:, :], preferred_element_type=F32))
        h2 = (jnp.dot(x_lo, wub_ref[:half, :], preferred_element_type=F32)
              + jnp.dot(x_hi, wub_ref[half:, :], preferred_element_type=F32))
        hid = ((h1 * _sigmoid(h1)) * h2).astype(BF16)
        for c, width in _col_chunks(wdb_ref.shape[1]):
            yb_ref[:, c:c + width] = jnp.dot(hid, wdb_ref[:, c:c + width], preferred_element_type=F32)

    @pl.when(jnp.logical_and(s >= 1, blk >= nu))
    def _():
        yb_ref[...] = jnp.zeros_like(yb_ref)

    last_blk = be_ref.shape[0] - 1
    changed = be_ref[jnp.minimum(s, last_blk)] != be_ref[jnp.clip(s - 1, 0, last_blk)]

    @pl.when(jnp.logical_and(s < nu, jnp.logical_or(s == 0, changed)))
    def _():
        _cast_rows(wg_ref, wgb_ref)
        _cast_rows(wu_ref, wub_ref)
        _cast_rows(wd_ref, wdb_ref)


def _experts(blk_exp, n_used, xb, wg, wu, wd, layer):
    d, de = wg.shape[-2:]
    rt_in = d // 2 // LANES
    n_blk = xb.shape[0] // (MOE_BLOCK * rt_in)

    def weight(s, be, nu):
        return (layer, be[jnp.minimum(s, nu[0] - 1)], 0, 0)

    return pl.pallas_call(
        _experts_kernel,
        grid_spec=pltpu.PrefetchScalarGridSpec(
            num_scalar_prefetch=2,
            grid=(n_blk + 1,),
            in_specs=[
                pl.BlockSpec((MOE_BLOCK * rt_in, LANES),
                             lambda s, be, nu: (jnp.clip(s - 1, 0, nu[0] - 1), 0)),
                pl.BlockSpec((None, None, d, de), weight),
                pl.BlockSpec((None, None, d, de), weight),
                pl.BlockSpec((None, None, de, d), weight),
            ],
            out_specs=pl.BlockSpec((MOE_BLOCK, d), lambda s, be, nu: (jnp.maximum(s - 1, 0), 0)),
            scratch_shapes=[pltpu.VMEM((d, de), BF16), pltpu.VMEM((d, de), BF16),
                            pltpu.VMEM((de, d), BF16)],
        ),
        out_shape=jax.ShapeDtypeStruct((n_blk * MOE_BLOCK, d), F32),
        compiler_params=_cparams("arbitrary"),
        name="moe_experts",
    )(blk_exp, n_used, xb, wg, wu, wd)


def _combine_kernel(*refs, final):
    if final:
        (d0_ref, d1_ref, d0n_ref, d1n_ref, x_ref, gate_ref, yb_ref, gf_ref,
         o_ref, a_ref, b_ref, sem) = refs
    else:
        d0_ref, d1_ref, d0n_ref, d1n_ref, x_ref, gate_ref, yb_ref, o_ref, a_ref, b_ref, sem = refs
    i = pl.program_id(0)
    last = pl.num_programs(0) - 1
    tm = x_ref.shape[0]
    slot = i % 2

    def gather(s0_ref, s1_ref, s):
        def issue(j, c):
            for u in range(DMA_UNROLL):
                r = j * DMA_UNROLL + u
                pltpu.make_async_copy(yb_ref.at[pl.ds(s0_ref[r], 1)], a_ref.at[s, pl.ds(r, 1)],
                                      sem.at[0, s]).start(priority=u % 2)
                pltpu.make_async_copy(yb_ref.at[pl.ds(s1_ref[r], 1)], b_ref.at[s, pl.ds(r, 1)],
                                      sem.at[1, s]).start(priority=(u + 1) % 2)
            return c

        lax.fori_loop(0, tm // DMA_UNROLL, issue, 0)

    @pl.when(i == 0)
    def _():
        gather(d0_ref, d1_ref, 0)

    @pl.when(i < last)
    def _():
        gather(d0n_ref, d1n_ref, 1 - slot)

    pltpu.make_async_copy(yb_ref.at[pl.ds(0, tm)], a_ref.at[slot], sem.at[0, slot]).wait()
    pltpu.make_async_copy(yb_ref.at[pl.ds(0, tm)], b_ref.at[slot], sem.at[1, slot]).wait()

    def body(j, c):
        r = pl.multiple_of(j * ROW_CHUNK, ROW_CHUNK)
        rs = pl.ds(r, ROW_CHUNK)
        gates = gate_ref[rs, :]
        y = gates[:, 0:1] * a_ref[slot, rs, :] + gates[:, 1:2] * b_ref[slot, rs, :]
        xn = x_ref[rs, :] + y
        if final:
            ms = jnp.mean(xn * xn, axis=-1, keepdims=True)
            xn = xn * lax.rsqrt(ms + EPS) * gf_ref[...]
        o_ref[rs, :] = xn
        return c

    lax.fori_loop(0, tm // ROW_CHUNK, body, 0)


def _combine(d0, d1, x, gates, yb, g_final):
    t, d = x.shape
    tm = _pick(t, (256, 128))
    n = t // tm
    final = g_final is not None

    def nxt(i):
        return (jnp.minimum(i + 1, n - 1),)

    in_specs = [
        pl.BlockSpec((tm,), lambda i: (i,), memory_space=pltpu.SMEM),
        pl.BlockSpec((tm,), lambda i: (i,), memory_space=pltpu.SMEM),
        pl.BlockSpec((tm,), nxt, memory_space=pltpu.SMEM),
        pl.BlockSpec((tm,), nxt, memory_space=pltpu.SMEM),
        pl.BlockSpec((tm, d), lambda i: (i, 0)),
        pl.BlockSpec((tm, LANES), lambda i: (i, 0)),
        pl.BlockSpec(memory_space=pl.ANY),
    ]
    args = [d0, d1, d0, d1, x, gates, yb]
    if final:
        in_specs.append(pl.BlockSpec((1, d), lambda i: (0, 0)))
        args.append(g_final)
    return pl.pallas_call(
        functools.partial(_combine_kernel, final=final),
        grid=(n,),
        in_specs=in_specs,
        out_specs=pl.BlockSpec((tm, d), lambda i: (i, 0)),
        out_shape=jax.ShapeDtypeStruct((t, d), F32),
        scratch_shapes=[pltpu.VMEM((2, tm, d), F32), pltpu.VMEM((2, tm, d), F32),
                        pltpu.SemaphoreType.DMA((2, 2))],
        compiler_params=_cparams("arbitrary", disable_bounds_checks=True),
        name="moe_combine",
    )(*args)


def _moe(x, g_norm, w_route, b_route, wg, wu, wd, layer, g_final, *, n_grp, epg):
    t, d = x.shape
    n_exp = n_grp * epg
    eid, rank, gates, cnt = _router(x, g_norm, w_route, b_route, n_grp=n_grp, epg=epg)

    counts = cnt[n_grp:n_grp + n_exp, 0].astype(I32)
    pblocks = (counts + MOE_BLOCK - 1) // MOE_BLOCK
    bends = jnp.cumsum(pblocks)
    pstarts = (bends - pblocks) * MOE_BLOCK
    n_blk = (t * TOP_K) // MOE_BLOCK + n_exp
    blk = jnp.arange(n_blk, dtype=I32)
    blk_exp = jnp.minimum(jnp.sum(bends[None, :] <= blk[:, None], axis=1), n_exp - 1).astype(I32)
    n_used = bends[-1:].astype(I32)
    last_blk = jnp.where(counts > 0, bends - 1, -1)
    trail = n_used + jnp.arange(n_exp, dtype=I32)
    zero_blocks = jnp.concatenate([last_blk, jnp.where(trail < n_blk, trail, -1)]).astype(I32)

    slots = _slots(pstarts.astype(I32), eid, rank)
    d0, d1 = slots[0], slots[1]
    xb = _dispatch(zero_blocks, d0, d1, x, g_norm, n_blk * MOE_BLOCK)
    yb = _experts(blk_exp, n_used, xb, wg, wu, wd, layer)
    return _combine(d0, d1, x, gates, yb, g_final)


def _row(v):
    return v.reshape(1, -1).astype(F32)


def _trunk(x, p):
    b, s, d = x.shape
    t = b * s
    depth = p["norm_mix"].shape[0]
    xt = x.reshape(t, d)
    for i in range(depth):
        j = i // 2
        if i % 2 == 0:
            z, kt, gates = _inproj_mlstm(xt, _row(p["norm_mix"][i]), p["mlstm_w_main"][j],
                                         p["mlstm_w_kt"][j], p["mlstm_w_gate"][j])
            ng = 4 * MLSTM_HEADS
            gcol = gates.reshape(b, s, N_GATE_LANES)
            grow = jnp.swapaxes(gcol[:, :, :ng], 1, 2)
            brow, bcol = p["mlstm_b_row"][j], p["mlstm_b_col"][j]
            z3 = z.reshape(b, s, -1)
            hf = _mlstm(z3, kt, gcol, grow, brow, bcol, reverse=False)
            hn = _mlstm(z3, kt, gcol, grow, brow, bcol, reverse=True,
                        fused=(hf, _row(p["mlstm_g_hnorm"][j])))
            xt = _outproj(hn.reshape(t, -1), p["mlstm_w_out"][j], xt)
        else:
            z, ssq = _inproj_sgu(xt, _row(p["norm_mix"][i]), p["sgu_w_in"][j])
            xt = _sgu_mix(z, ssq, _row(p["sgu_g_v"][j]), p["sgu_w_s"][j], p["sgu_b_s"][j],
                          p["sgu_w_out"][j], xt)
        g_final = _row(p["norm_final"]) if i == depth - 1 else None
        xt = _moe(xt, _row(p["norm_ffn"][i]), p["moe_w_route"][i], p["moe_b_route"][i],
                  p["moe_w_gate"], p["moe_w_up"], p["moe_w_down"], i, g_final,
                  n_grp=p["n_grp"], epg=p["epg"])
    if depth == 0:
        raise ValueError("depth 0 is not supported")
    return xt.reshape(b, s, d)


def kernel(x_prompt, x_sample, norm_mix, norm_ffn, norm_final, mlstm_w_in, mlstm_b_gates, mlstm_g_hnorm, mlstm_w_out, sgu_w_in, sgu_g_v, sgu_w_s, sgu_b_s, sgu_w_out, moe_w_grp, moe_b_grp, moe_w_exp, moe_b_exp, moe_w_gate, moe_w_up, moe_w_down):
    ng = 4 * MLSTM_HEADS
    n_main = mlstm_w_in.shape[-1] - ng
    qk_dim = (n_main - 2 * mlstm_w_out.shape[-2]) // 2
    n_grp = moe_w_grp.shape[-1]
    n_exp = moe_w_exp.shape[-1]
    assert n_grp + n_exp <= LANES and ng <= N_GATE_LANES

    def pad_lanes(a, width):
        return jnp.pad(a, [(0, 0)] * (a.ndim - 1) + [(0, width - a.shape[-1])])

    p = dict(
        norm_mix=norm_mix, norm_ffn=norm_ffn, norm_final=norm_final,
        mlstm_w_main=jnp.concatenate([mlstm_w_in[..., 2 * qk_dim:n_main], mlstm_w_in[..., :qk_dim]],
                                     axis=-1).astype(BF16),
        mlstm_w_kt=jnp.swapaxes(mlstm_w_in[..., qk_dim:2 * qk_dim], -1, -2).astype(BF16),
        mlstm_w_gate=pad_lanes(mlstm_w_in[..., n_main:], N_GATE_LANES).astype(BF16),
        mlstm_b_row=pad_lanes(mlstm_b_gates, N_GATE_LANES)[:, None, :].astype(F32),
        mlstm_b_col=jnp.broadcast_to(mlstm_b_gates[:, :, None], mlstm_b_gates.shape + (MLSTM_CHUNK,)).astype(F32),
        mlstm_g_hnorm=mlstm_g_hnorm,
        mlstm_w_out=mlstm_w_out.astype(BF16),
        sgu_w_in=sgu_w_in.astype(BF16),
        sgu_g_v=sgu_g_v,
        sgu_w_s=sgu_w_s.astype(BF16),
        sgu_b_s=jnp.broadcast_to(sgu_b_s[..., None], sgu_b_s.shape + (LANES,)).astype(F32),
        sgu_w_out=sgu_w_out.astype(BF16),
        moe_w_route=pad_lanes(jnp.concatenate([moe_w_grp, moe_w_exp], axis=-1), LANES).astype(BF16),
        moe_b_route=pad_lanes(jnp.concatenate([moe_b_grp, moe_b_exp], axis=-1), LANES)[:, None, :].astype(F32),
        moe_w_gate=moe_w_gate, moe_w_up=moe_w_up, moe_w_down=moe_w_down,
        n_grp=n_grp, epg=n_exp // n_grp,
    )
    return (_trunk(x_prompt, p), _trunk(x_sample, p))
```

```python
import functools

import jax
import jax.numpy as jnp
from jax import lax
from jax.experimental import pallas as pl
from jax.experimental.pallas import tpu as pltpu

F32 = jnp.float32
BF16 = jnp.bfloat16
U32 = jnp.uint32
I32 = jnp.int32

EPS = 1e-6
MLSTM_HEADS = 4
MLSTM_CHUNK = 128
HEAD_GROUP = 4
GATE_SOFTCAP = 15.0
N_GATE_LANES = 128
SGU_CHUNK = 128
TOP_K = 2
ROW_CHUNK = 128
MOE_BLOCK = 512
DMA_UNROLL = 8
LANES = 128
VMEM_LIMIT_BYTES = 56 * 1024 * 1024


def _cparams(*sem, **kw):
    return pltpu.CompilerParams(dimension_semantics=sem, vmem_limit_bytes=VMEM_LIMIT_BYTES, **kw)


def _pick(n, prefs):
    for p in prefs:
        if n % p == 0:
            return p
    return n


def _rmsnorm_rows(x_ref, g_ref, out_ref):
    rows = x_ref.shape[0]
    g = g_ref[...]

    def body(i, c):
        r = pl.multiple_of(i * ROW_CHUNK, ROW_CHUNK)
        x = x_ref[pl.ds(r, ROW_CHUNK), :]
        ms = jnp.mean(x * x, axis=-1, keepdims=True)
        out_ref[pl.ds(r, ROW_CHUNK), :] = (x * lax.rsqrt(ms + EPS) * g).astype(out_ref.dtype)
        return c

    lax.fori_loop(0, rows // ROW_CHUNK, body, 0)


def _sigmoid(x):
    return 1.0 / (1.0 + jnp.exp(-x))


def _log_sigmoid(x):
    return jnp.minimum(x, 0.0) - jnp.log1p(jnp.exp(-jnp.abs(x)))


def _soft_cap(t):
    return GATE_SOFTCAP * jnp.tanh(t / GATE_SOFTCAP)


def _split3(a):
    hi = a.astype(BF16)
    r1 = a - hi.astype(F32)
    mid = r1.astype(BF16)
    lo = (r1 - mid.astype(F32)).astype(BF16)
    return [hi, mid, lo]


def _resident(shape):
    return pl.BlockSpec(shape, lambda i: (0,) * len(shape), pipeline_mode=pl.Buffered(1))


def _col_chunks(n):
    step = _pick(n, (1024, 512, 256, 128))
    return [(c, step) for c in range(0, n, step)]


def _inproj_mlstm_kernel(x_ref, g_ref, w_ref, wkt_ref, wg_ref, z_ref, kt_ref, gate_ref, xn_ref):
    _rmsnorm_rows(x_ref, g_ref, xn_ref)
    xn = xn_ref[...]
    gate_ref[...] = jnp.dot(xn, wg_ref[...], preferred_element_type=F32)
    kt_ref[...] = lax.dot_general(wkt_ref[...], xn, (((1,), (1,)), ((), ())),
                                  preferred_element_type=F32).astype(kt_ref.dtype)
    for c, w in _col_chunks(w_ref.shape[1]):
        z_ref[:, c:c + w] = jnp.dot(xn, w_ref[:, c:c + w], preferred_element_type=F32).astype(z_ref.dtype)


def _inproj_mlstm(x, g, w, wkt, wg):
    t, d = x.shape
    n = w.shape[1]
    qk = wkt.shape[0]
    tm = _pick(t, (512, 256, 128))
    return pl.pallas_call(
        _inproj_mlstm_kernel,
        grid=(t // tm,),
        in_specs=[
            pl.BlockSpec((tm, d), lambda i: (i, 0)),
            _resident((1, d)),
            _resident((d, n)),
            _resident((qk, d)),
            _resident((d, N_GATE_LANES)),
        ],
        out_specs=[
            pl.BlockSpec((tm, n), lambda i: (i, 0)),
            pl.BlockSpec((qk, tm), lambda i: (0, i)),
            pl.BlockSpec((tm, N_GATE_LANES), lambda i: (i, 0)),
        ],
        out_shape=[
            jax.ShapeDtypeStruct((t, n), BF16),
            jax.ShapeDtypeStruct((qk, t), BF16),
            jax.ShapeDtypeStruct((t, N_GATE_LANES), F32),
        ],
        scratch_shapes=[pltpu.VMEM((tm, d), BF16)],
        compiler_params=_cparams("parallel"),
        name="inproj_mlstm",
    )(x, g, w, wkt, wg)


def _mlstm_kernel(*refs, reverse, fuse, heads, dk, dv):
    n_in = 10 if fuse else 7
    if fuse:
        q_ref, kt_ref, v_ref, gcol_ref, grow_ref, brow_ref, bcol_ref, o_ref, hf_ref, gh_ref = refs[:n_in]
    else:
        q_ref, kt_ref, v_ref, gcol_ref, grow_ref, brow_ref, bcol_ref = refs[:n_in]
    out_ref = refs[n_in]
    state = refs[n_in + 1:]
    c_refs, m_refs = state[0::2], state[1::2]
    L = MLSTM_CHUNK

    @pl.when(pl.program_id(1) == 0)
    def _():
        for ref in state:
            ref[...] = jnp.zeros_like(ref)

    gc = _soft_cap(gcol_ref[...] + brow_ref[...])
    gr = _soft_cap(grow_ref[...] + bcol_ref[...])
    lsc = _log_sigmoid(gc)
    lsr = _log_sigmoid(gr)
    row = lax.broadcasted_iota(I32, (L, L), 0)
    col = lax.broadcasted_iota(I32, (L, L), 1)
    lower = col <= row
    upper = col >= row
    mask_t, mask_s = (upper, lower) if reverse else (lower, upper)
    ng = gr.shape[0]
    incl_t = jnp.where(mask_t, 1.0, 0.0).astype(BF16)
    incl_s = jnp.where(mask_s, 1.0, 0.0).astype(BF16)
    cc = jnp.dot(incl_t, jnp.concatenate(_split3(lsc), axis=1), preferred_element_type=F32)
    cum_cols = cc[:, :LANES] + cc[:, LANES:2 * LANES] + cc[:, 2 * LANES:]
    cr = jnp.dot(jnp.concatenate(_split3(lsr), axis=0), incl_s, preferred_element_type=F32)
    cum_rows = cr[:ng] + cr[ng:2 * ng] + cr[2 * ng:]
    end = 0 if reverse else L - 1
    scale = dk ** -0.5
    reps = dv // LANES

    def lanes(a, n):
        return jnp.concatenate([a] * n, axis=1) if n > 1 else a

    def first(h):
        ci = (2 * heads if reverse else 0) + h
        cf = ci + heads
        i_col = jnp.broadcast_to(gc[:, ci:ci + 1], (L, LANES))
        b_col = jnp.broadcast_to(cum_cols[:, cf:cf + 1], (L, LANES))
        i_row = gr[ci:ci + 1, :]
        b_row = cum_rows[cf:cf + 1, :]
        g = b_col[end:end + 1, :]
        m_prev = m_refs[h][...]
        a_col = g - b_col + i_col
        m_new = jnp.maximum(g + m_prev, jnp.max(a_col, axis=0, keepdims=True))
        decay = jnp.exp(g + m_prev - m_new)
        w_row = jnp.exp(g - b_row + i_row - m_new)

        q = q_ref[:, h * dk:(h + 1) * dk]
        kt = kt_ref[h * dk:(h + 1) * dk, :]
        qs_b = (q.astype(F32) * scale).astype(BF16)
        cx_old = c_refs[h][...]
        num_x = jnp.dot(qs_b, cx_old.astype(BF16), preferred_element_type=F32)
        qk = jnp.dot(qs_b, kt, preferred_element_type=F32)

        log_inter = b_col + m_prev
        dlog = jnp.where(mask_t, b_col - b_row + i_row, -jnp.inf)
        m_t = jnp.maximum(log_inter, jnp.max(dlog, axis=1, keepdims=True))
        return m_new, decay, w_row, kt, cx_old, num_x, qk, log_inter, dlog, m_t

    ones = jnp.ones((L, LANES), BF16)

    def second(h, staged):
        m_new, decay, w_row, kt, cx_old, num_x, qk, log_inter, dlog, m_t = staged
        vx = jnp.concatenate([v_ref[:, h * dv:(h + 1) * dv], ones], axis=1)
        s = qk * jnp.exp(dlog - m_t)
        inter = jnp.exp(log_inter - m_t)
        sv = jnp.dot(s.astype(BF16), vx, preferred_element_type=F32)
        num = lanes(inter, reps) * num_x[:, :dv] + sv[:, :dv]
        den = inter * num_x[:, dv:] + sv[:, dv:]
        hh = num * lanes(1.0 / jnp.maximum(jnp.abs(den), jnp.exp(-m_t)), reps)

        kwt = (kt.astype(F32) * w_row).astype(BF16)
        c_refs[h][...] = lanes(decay, reps + 1) * cx_old + jnp.dot(kwt, vx, preferred_element_type=F32)
        m_refs[h][...] = m_new

        hs = slice(h * dv, (h + 1) * dv)
        if fuse:
            tot = hf_ref[:, hs] + hh
            ms = jnp.mean(tot * tot, axis=-1, keepdims=True)
            gated = tot * lax.rsqrt(ms + EPS) * gh_ref[:, hs] * _sigmoid(o_ref[:, hs].astype(F32))
            out_ref[:, hs] = gated.astype(out_ref.dtype)
        else:
            out_ref[:, hs] = hh

    for h0 in range(0, heads, HEAD_GROUP):
        group = range(h0, min(h0 + HEAD_GROUP, heads))
        staged = [first(h) for h in group]
        for h, st in zip(group, staged):
            second(h, st)


def _mlstm(z, kt, gcol, grow, brow, bcol, *, reverse, fused=None):
    b, s, width = z.shape
    heads = MLSTM_HEADS
    qk = kt.shape[0]
    vdim = (width - qk) // 2
    dk, dv = qk // heads, vdim // heads
    nc = s // MLSTM_CHUNK
    L = MLSTM_CHUNK
    ng = grow.shape[1]
    assert L == N_GATE_LANES and dv % LANES == 0 and (2 * vdim) % qk == 0

    def ch(c):
        return nc - 1 - c if reverse else c

    in_specs = [
        pl.BlockSpec((None, L, qk), lambda bi, c: (bi, ch(c), (2 * vdim) // qk)),
        pl.BlockSpec((qk, L), lambda bi, c: (0, bi * nc + ch(c))),
        pl.BlockSpec((None, L, vdim), lambda bi, c: (bi, ch(c), 0)),
        pl.BlockSpec((None, L, N_GATE_LANES), lambda bi, c: (bi, ch(c), 0)),
        pl.BlockSpec((None, ng, L), lambda bi, c: (bi, 0, ch(c))),
        pl.BlockSpec((1, N_GATE_LANES), lambda bi, c: (0, 0)),
        pl.BlockSpec((ng, L), lambda bi, c: (0, 0)),
    ]
    args = [z, kt, z, gcol, grow, brow, bcol]
    if fused is not None:
        hf, gh = fused
        in_specs += [
            pl.BlockSpec((None, L, vdim), lambda bi, c: (bi, ch(c), 1)),
            pl.BlockSpec((None, L, vdim), lambda bi, c: (bi, ch(c), 0)),
            pl.BlockSpec((1, vdim), lambda bi, c: (0, 0)),
        ]
        args += [z, hf, gh]
    return pl.pallas_call(
        functools.partial(_mlstm_kernel, reverse=reverse, fuse=fused is not None,
                          heads=heads, dk=dk, dv=dv),
        grid=(b, nc),
        in_specs=in_specs,
        out_specs=pl.BlockSpec((None, L, vdim), lambda bi, c: (bi, ch(c), 0)),
        out_shape=jax.ShapeDtypeStruct((b, s, vdim), BF16 if fused is not None else F32),
        scratch_shapes=[pltpu.VMEM((dk, dv + LANES), F32), pltpu.VMEM((1, LANES), F32)] * heads,
        compiler_params=_cparams("parallel", "arbitrary"),
        name="mlstm_bwd" if reverse else "mlstm_fwd",
    )(*args)


def _outproj_kernel(h_ref, w_ref, x_ref, o_ref):
    h = h_ref[...]
    for c, w in _col_chunks(w_ref.shape[1]):
        o_ref[:, c:c + w] = x_ref[:, c:c + w] + jnp.dot(h, w_ref[:, c:c + w], preferred_element_type=F32)


def _outproj(h, w, x):
    t, kdim = h.shape
    n = w.shape[1]
    tm = _pick(t, (512, 256, 128))
    return pl.pallas_call(
        _outproj_kernel,
        grid=(t // tm,),
        in_specs=[
            pl.BlockSpec((tm, kdim), lambda i: (i, 0)),
            _resident((kdim, n)),
            pl.BlockSpec((tm, n), lambda i: (i, 0)),
        ],
        out_specs=pl.BlockSpec((tm, n), lambda i: (i, 0)),
        out_shape=jax.ShapeDtypeStruct((t, n), F32),
        compiler_params=_cparams("parallel"),
        name="outproj",
    )(h, w, x)


def _inproj_sgu_kernel(x_ref, g_ref, w_ref, z_ref, ssq_ref, xn_ref):
    _rmsnorm_rows(x_ref, g_ref, xn_ref)
    xn = xn_ref[...]
    n = w_ref.shape[1]
    part = None
    for c, w in _col_chunks(n // 2) + [(n // 2 + c, w) for c, w in _col_chunks(n // 2)]:
        z = jax.nn.gelu(jnp.dot(xn, w_ref[:, c:c + w], preferred_element_type=F32))
        z_ref[:, c:c + w] = z.astype(z_ref.dtype)
        if c >= n // 2:
            z2 = z * z
            for l in range(0, w, LANES):
                part = z2[:, l:l + LANES] if part is None else part + z2[:, l:l + LANES]
    ssq_ref[...] = part


def _inproj_sgu(x, g, w):
    t, d = x.shape
    n = w.shape[1]
    tm = _pick(t, (512, 256, 128))
    return pl.pallas_call(
        _inproj_sgu_kernel,
        grid=(t // tm,),
        in_specs=[
            pl.BlockSpec((tm, d), lambda i: (i, 0)),
            _resident((1, d)),
            _resident((d, n)),
        ],
        out_specs=[
            pl.BlockSpec((tm, n), lambda i: (i, 0)),
            pl.BlockSpec((tm, LANES), lambda i: (i, 0)),
        ],
        out_shape=[
            jax.ShapeDtypeStruct((t, n), BF16),
            jax.ShapeDtypeStruct((t, LANES), F32),
        ],
        scratch_shapes=[pltpu.VMEM((tm, d), BF16)],
        compiler_params=_cparams("parallel"),
        name="inproj_sgu",
    )(x, g, w)


def _sgu_mix_kernel(u_ref, v_ref, ssq_ref, gv_ref, ws_ref, bs_ref, wo_ref, x_ref, o_ref, gated_ref,
                    *, groups, gd):
    rows = u_ref.shape[0]
    width = v_ref.shape[1]
    gv = gv_ref[...]
    for c in range(rows // SGU_CHUNK):
        rs = slice(c * SGU_CHUNK, (c + 1) * SGU_CHUNK)
        ms = jnp.sum(ssq_ref[rs, :], axis=-1, keepdims=True) / width
        vn = (v_ref[rs, :].astype(F32) * lax.rsqrt(ms + EPS) * gv).astype(BF16)
        for gi in range(groups):
            cs = slice(gi * gd, (gi + 1) * gd)
            mixed = jnp.dot(ws_ref[gi], vn[:, cs], preferred_element_type=F32)
            bias = bs_ref[gi]
            bias = jnp.concatenate([bias] * (gd // LANES), axis=1) if gd > LANES else bias
            gated_ref[rs, cs] = (u_ref[rs, cs].astype(F32) * (mixed + bias)).astype(BF16)
    gated = gated_ref[...]
    for c, w in _col_chunks(wo_ref.shape[1]):
        o_ref[:, c:c + w] = x_ref[:, c:c + w] + jnp.dot(gated, wo_ref[:, c:c + w], preferred_element_type=F32)


def _sgu_mix(z, ssq, gv, ws, bs, wo, x):
    t, d = x.shape
    width = z.shape[1] // 2
    groups = ws.shape[0]
    tm = _pick(t, (512, 256, 128))
    return pl.pallas_call(
        functools.partial(_sgu_mix_kernel, groups=groups, gd=width // groups),
        grid=(t // tm,),
        in_specs=[
            pl.BlockSpec((tm, width), lambda i: (i, 0)),
            pl.BlockSpec((tm, width), lambda i: (i, 1)),
            pl.BlockSpec((tm, LANES), lambda i: (i, 0)),
            _resident((1, width)),
            _resident((groups, SGU_CHUNK, SGU_CHUNK)),
            _resident((groups, SGU_CHUNK, LANES)),
            _resident((width, d)),
            pl.BlockSpec((tm, d), lambda i: (i, 0)),
        ],
        out_specs=pl.BlockSpec((tm, d), lambda i: (i, 0)),
        out_shape=jax.ShapeDtypeStruct((t, d), F32),
        scratch_shapes=[pltpu.VMEM((tm, width), BF16)],
        compiler_params=_cparams("parallel"),
        name="sgu_mix",
    )(z, z, ssq, gv, ws, bs, wo, x)


def _router_kernel(x_ref, g_ref, w_ref, b_ref, eid_ref, rank_ref, gate_ref, cnt_ref, hn_ref, carry_ref,
                   *, n_grp, epg, rows):
    tm = x_ref.shape[0]

    @pl.when(pl.program_id(0) == 0)
    def _():
        carry_ref[...] = jnp.zeros_like(carry_ref)

    _rmsnorm_rows(x_ref, g_ref, hn_ref)
    logits = jnp.dot(hn_ref[...], w_ref[...], preferred_element_type=F32) + b_ref[...]
    lt = logits.T[:rows, :]
    ridx = lax.broadcasted_iota(I32, (rows, tm), 0).astype(F32)
    big = float(rows)

    gl = jnp.where(ridx < n_grp, lt, -jnp.inf)
    gmax = jnp.max(gl, axis=0, keepdims=True)
    grp = jnp.min(jnp.where(gl == gmax, ridx, big), axis=0, keepdims=True)
    grp_p = 1.0 / jnp.sum(jnp.exp(gl - gmax), axis=0, keepdims=True)

    lo = n_grp + grp * epg
    el = jnp.where(ridx >= lo, jnp.where(ridx < lo + epg, lt, -jnp.inf), -jnp.inf)
    emax = jnp.max(el, axis=0, keepdims=True)
    ee = jnp.exp(el - emax)
    p = ee / jnp.sum(ee, axis=0, keepdims=True)
    pm = jnp.where(ridx >= lo, jnp.where(ridx < lo + epg, p, -1.0), -1.0)
    p1 = jnp.max(pm, axis=0, keepdims=True)
    i1 = jnp.min(jnp.where(pm == p1, ridx, big), axis=0, keepdims=True)
    pm2 = jnp.where(ridx == i1, -1.0, pm)
    p2 = jnp.max(pm2, axis=0, keepdims=True)
    i2 = jnp.min(jnp.where(pm2 == p2, ridx, big), axis=0, keepdims=True)
    psum = p1 + p2
    g1 = grp_p * (p1 / psum)
    g2 = grp_p * (p2 / psum)

    hit1 = ridx == i1
    hit2 = ridx == i2
    onehot = jnp.where(hit1, 1.0, jnp.where(hit2, 1.0, 0.0))
    r = lax.broadcasted_iota(I32, (tm, tm), 0)
    c = lax.broadcasted_iota(I32, (tm, tm), 1)
    earlier = jnp.where(r < c, 1.0, 0.0).astype(BF16)
    carry = carry_ref[:, 0:1]
    before = jnp.dot(onehot.astype(BF16), earlier, preferred_element_type=F32) + carry
    rank1 = jnp.sum(jnp.where(hit1, before, 0.0), axis=0, keepdims=True)
    rank2 = jnp.sum(jnp.where(hit2, before, 0.0), axis=0, keepdims=True)
    carry = carry + jnp.sum(onehot, axis=1, keepdims=True)
    carry_ref[...] = jnp.broadcast_to(carry, carry_ref.shape)
    cnt_ref[...] = jnp.broadcast_to(carry, cnt_ref.shape)

    sub = lax.broadcasted_iota(I32, (8, tm), 0)
    eid_ref[...] = jnp.where(sub == 0, i1 - n_grp, jnp.where(sub == 1, i2 - n_grp, 0.0)).astype(I32)
    rank_ref[...] = jnp.where(sub == 0, rank1, jnp.where(sub == 1, rank2, 0.0)).astype(I32)
    sub_l = lax.broadcasted_iota(I32, (LANES, tm), 0)
    gate_ref[...] = jnp.where(sub_l == 0, g1, jnp.where(sub_l == 1, g2, 0.0)).T


def _router(x, g, w, b, *, n_grp, epg):
    t, d = x.shape
    tm = _pick(t, (512, 256, 128))
    rows = -(-(n_grp + n_grp * epg) // 8) * 8
    return pl.pallas_call(
        functools.partial(_router_kernel, n_grp=n_grp, epg=epg, rows=rows),
        grid=(t // tm,),
        in_specs=[
            pl.BlockSpec((tm, d), lambda i: (i, 0)),
            pl.BlockSpec((1, d), lambda i: (0, 0)),
            pl.BlockSpec((d, LANES), lambda i: (0, 0)),
            pl.BlockSpec((1, LANES), lambda i: (0, 0)),
        ],
        out_specs=[
            pl.BlockSpec((8, tm), lambda i: (0, i)),
            pl.BlockSpec((8, tm), lambda i: (0, i)),
            pl.BlockSpec((tm, LANES), lambda i: (i, 0)),
            pl.BlockSpec((rows, LANES), lambda i: (0, 0)),
        ],
        out_shape=[
            jax.ShapeDtypeStruct((8, t), I32),
            jax.ShapeDtypeStruct((8, t), I32),
            jax.ShapeDtypeStruct((t, LANES), F32),
            jax.ShapeDtypeStruct((rows, LANES), F32),
        ],
        scratch_shapes=[pltpu.VMEM((tm, d), BF16), pltpu.VMEM((rows, LANES), F32)],
        compiler_params=_cparams("arbitrary"),
        name="router",
    )(x, g, w, b)


def _slots_kernel(ps_ref, eid_ref, rank_ref, slot_ref, *, n_exp):
    eid = eid_ref[...]
    base = jnp.zeros_like(eid)
    for e in range(n_exp):
        base = jnp.where(eid == e, ps_ref[e], base)
    slot_ref[...] = base + rank_ref[...]


def _slots(pstarts, eid, rank):
    rows, t = eid.shape
    tm = _pick(t, (4096, 2048, 1024, 512, 256, 128))
    return pl.pallas_call(
        functools.partial(_slots_kernel, n_exp=pstarts.shape[0]),
        grid_spec=pltpu.PrefetchScalarGridSpec(
            num_scalar_prefetch=1,
            grid=(t // tm,),
            in_specs=[
                pl.BlockSpec((rows, tm), lambda i, ps: (0, i)),
                pl.BlockSpec((rows, tm), lambda i, ps: (0, i)),
            ],
            out_specs=pl.BlockSpec((rows, tm), lambda i, ps: (0, i)),
        ),
        out_shape=jax.ShapeDtypeStruct((rows, t), I32),
        compiler_params=_cparams("parallel"),
        name="moe_slots",
    )(pstarts, eid, rank)


def _dispatch_kernel(zb_ref, d0_ref, d1_ref, x_ref, g_ref, xb_ref, pk_ref, zero_ref, sem, zsem):
    i = pl.program_id(0)
    last = pl.num_programs(0) - 1
    tm, d = x_ref.shape
    half = d // 2
    rt = half // LANES
    slot = i % 2
    g = g_ref[...]

    @pl.when(i == 0)
    def _():
        zero_ref[...] = jnp.zeros_like(zero_ref)

        def zero_copy(j):
            lines = MOE_BLOCK * rt
            start = pl.multiple_of(jnp.maximum(zb_ref[j], 0) * lines, lines)
            return pltpu.make_async_copy(zero_ref, xb_ref.at[pl.ds(start, lines)], zsem.at[0])

        def zstart(j, c):
            @pl.when(zb_ref[j] >= 0)
            def _():
                zero_copy(j).start()
            return c

        def zwait(j, c):
            @pl.when(zb_ref[j] >= 0)
            def _():
                zero_copy(j).wait()
            return c

        lax.fori_loop(0, zb_ref.shape[0], zstart, 0)
        lax.fori_loop(0, zb_ref.shape[0], zwait, 0)

    def pack(j, c):
        r = pl.multiple_of(j * ROW_CHUNK, ROW_CHUNK)
        x = x_ref[pl.ds(r, ROW_CHUNK), :]
        ms = jnp.mean(x * x, axis=-1, keepdims=True)
        hn = (x * lax.rsqrt(ms + EPS) * g).astype(BF16).astype(F32)
        bits = lax.bitcast_convert_type(hn, U32)
        lo = lax.shift_right_logical(bits[:, :half], jnp.uint32(16))
        hi = bits[:, half:] & jnp.uint32(0xFFFF0000)
        words = hi | lo
        for s in range(rt):
            pk_ref[slot, pl.ds(r * rt + s, ROW_CHUNK, stride=rt), :] = words[:, s * LANES:(s + 1) * LANES]
        return c

    lax.fori_loop(0, tm // ROW_CHUNK, pack, 0)

    def row_copy(r, dest, prio):
        src = pk_ref.at[slot, pl.ds(pl.multiple_of(r * rt, rt), rt)]
        dst = xb_ref.at[pl.ds(pl.multiple_of(dest * rt, rt), rt)]
        pltpu.make_async_copy(src, dst, sem.at[slot]).start(priority=prio)

    def issue(j, c):
        for u in range(DMA_UNROLL):
            r = j * DMA_UNROLL + u
            row_copy(r, d0_ref[r], u % 2)
            row_copy(r, d1_ref[r], (u + 1) % 2)
        return c

    lax.fori_loop(0, tm // DMA_UNROLL, issue, 0)

    def drain(s):
        for _ in range(TOP_K):
            pltpu.make_async_copy(pk_ref.at[s], xb_ref.at[pl.ds(0, tm * rt)], sem.at[s]).wait()

    @pl.when(i > 0)
    def _():
        drain(1 - slot)

    @pl.when(i == last)
    def _():
        drain(slot)


def _dispatch(zero_blocks, d0, d1, x, g, n_rows):
    t, d = x.shape
    tm = _pick(t, (512, 256, 128))
    rt = d // 2 // LANES
    return pl.pallas_call(
        _dispatch_kernel,
        grid_spec=pltpu.PrefetchScalarGridSpec(
            num_scalar_prefetch=1,
            grid=(t // tm,),
            in_specs=[
                pl.BlockSpec((tm,), lambda i, zb: (i,), memory_space=pltpu.SMEM),
                pl.BlockSpec((tm,), lambda i, zb: (i,), memory_space=pltpu.SMEM),
                pl.BlockSpec((tm, d), lambda i, zb: (i, 0)),
                pl.BlockSpec((1, d), lambda i, zb: (0, 0)),
            ],
            out_specs=pl.BlockSpec(memory_space=pl.ANY),
            scratch_shapes=[pltpu.VMEM((2, tm * rt, LANES), U32), pltpu.VMEM((MOE_BLOCK * rt, LANES), U32),
                            pltpu.SemaphoreType.DMA((2,)), pltpu.SemaphoreType.DMA((1,))],
        ),
        out_shape=jax.ShapeDtypeStruct((n_rows * rt, LANES), U32),
        compiler_params=_cparams("arbitrary", disable_bounds_checks=True),
        name="moe_dispatch",
    )(zero_blocks, d0, d1, x, g)


def _cast_rows(src_ref, dst_ref):
    rows = src_ref.shape[0]
    step = _pick(rows, (256, 128))

    def body(j, c):
        r = pl.multiple_of(j * step, step)
        dst_ref[pl.ds(r, step), :] = src_ref[pl.ds(r, step), :].astype(dst_ref.dtype)
        return c

    lax.fori_loop(0, rows // step, body, 0)


def _experts_kernel(be_ref, nu_ref, xb_ref, wg_ref, wu_ref, wd_ref, yb_ref, wgb_ref, wub_ref, wdb_ref):
    s = pl.program_id(0)
    nu = nu_ref[0]
    blk = s - 1
    half = wgb_ref.shape[0] // 2
    rt_in = half // LANES

    @pl.when(jnp.logical_and(s >= 1, blk < nu))
    def _():
        w = jnp.concatenate([xb_ref[pl.ds(t, MOE_BLOCK, stride=rt_in), :] for t in range(rt_in)], axis=1)
        x_lo = lax.bitcast_convert_type(lax.shift_left(w, jnp.uint32(16)), F32).astype(BF16)
        x_hi = lax.bitcast_convert_type(w & jnp.uint32(0xFFFF0000), F32).astype(BF16)
        h1 = (jnp.dot(x_lo, wgb_ref[:half, :], preferred_element_type=F32)
              + jnp.dot(x_hi, wgb_ref[half:, :], preferred_element_type=F32))
        h2 = (jnp.dot(x_lo, wub_ref[:half, :], preferred_element_type=F32)
              + jnp.dot(x_hi, wub_ref[half:, :], preferred_element_type=F32))
        hid = ((h1 * _sigmoid(h1)) * h2).astype(BF16)
        for c, width in _col_chunks(wdb_ref.shape[1]):
            yb_ref[:, c:c + width] = jnp.dot(hid, wdb_ref[:, c:c + width], preferred_element_type=F32)

    @pl.when(jnp.logical_and(s >= 1, blk >= nu))
    def _():
        yb_ref[...] = jnp.zeros_like(yb_ref)

    last_blk = be_ref.shape[0] - 1
    changed = be_ref[jnp.minimum(s, last_blk)] != be_ref[jnp.clip(s - 1, 0, last_blk)]

    @pl.when(jnp.logical_and(s < nu, jnp.logical_or(s == 0, changed)))
    def _():
        _cast_rows(wg_ref, wgb_ref)
        _cast_rows(wu_ref, wub_ref)
        _cast_rows(wd_ref, wdb_ref)


def _experts(blk_exp, n_used, xb, wg, wu, wd, layer):
    d, de = wg.shape[-2:]
    rt_in = d // 2 // LANES
    n_blk = xb.shape[0] // (MOE_BLOCK * rt_in)

    def weight(s, be, nu):
        return (layer, be[jnp.minimum(s, nu[0] - 1)], 0, 0)

    return pl.pallas_call(
        _experts_kernel,
        grid_spec=pltpu.PrefetchScalarGridSpec(
            num_scalar_prefetch=2,
            grid=(n_blk + 1,),
            in_specs=[
                pl.BlockSpec((MOE_BLOCK * rt_in, LANES),
                             lambda s, be, nu: (jnp.clip(s - 1, 0, nu[0] - 1), 0)),
                pl.BlockSpec((None, None, d, de), weight),
                pl.BlockSpec((None, None, d, de), weight),
                pl.BlockSpec((None, None, de, d), weight),
            ],
            out_specs=pl.BlockSpec((MOE_BLOCK, d), lambda s, be, nu: (jnp.maximum(s - 1, 0), 0)),
            scratch_shapes=[pltpu.VMEM((d, de), BF16), pltpu.VMEM((d, de), BF16),
                            pltpu.VMEM((de, d), BF16)],
        ),
        out_shape=jax.ShapeDtypeStruct((n_blk * MOE_BLOCK, d), F32),
        compiler_params=_cparams("arbitrary"),
        name="moe_experts",
    )(blk_exp, n_used, xb, wg, wu, wd)


def _combine_kernel(*refs, final):
    if final:
        (d0_ref, d1_ref, d0n_ref, d1n_ref, x_ref, gate_ref, yb_ref, gf_ref,
         o_ref, a_ref, b_ref, sem) = refs
    else:
        d0_ref, d1_ref, d0n_ref, d1n_ref, x_ref, gate_ref, yb_ref, o_ref, a_ref, b_ref, sem = refs
    i = pl.program_id(0)
    last = pl.num_programs(0) - 1
    tm = x_ref.shape[0]
    slot = i % 2

    def gather(s0_ref, s1_ref, s):
        def issue(j, c):
            for u in range(DMA_UNROLL):
                r = j * DMA_UNROLL + u
                pltpu.make_async_copy(yb_ref.at[pl.ds(s0_ref[r], 1)], a_ref.at[s, pl.ds(r, 1)],
                                      sem.at[0, s]).start(priority=u % 2)
                pltpu.make_async_copy(yb_ref.at[pl.ds(s1_ref[r], 1)], b_ref.at[s, pl.ds(r, 1)],
                                      sem.at[1, s]).start(priority=(u + 1) % 2)
            return c

        lax.fori_loop(0, tm // DMA_UNROLL, issue, 0)

    @pl.when(i == 0)
    def _():
        gather(d0_ref, d1_ref, 0)

    @pl.when(i < last)
    def _():
        gather(d0n_ref, d1n_ref, 1 - slot)

    pltpu.make_async_copy(yb_ref.at[pl.ds(0, tm)], a_ref.at[slot], sem.at[0, slot]).wait()
    pltpu.make_async_copy(yb_ref.at[pl.ds(0, tm)], b_ref.at[slot], sem.at[1, slot]).wait()

    def body(j, c):
        r = pl.multiple_of(j * ROW_CHUNK, ROW_CHUNK)
        rs = pl.ds(r, ROW_CHUNK)
        gates = gate_ref[rs, :]
        y = gates[:, 0:1] * a_ref[slot, rs, :] + gates[:, 1:2] * b_ref[slot, rs, :]
        xn = x_ref[rs, :] + y
        if final:
            ms = jnp.mean(xn * xn, axis=-1, keepdims=True)
            xn = xn * lax.rsqrt(ms + EPS) * gf_ref[...]
        o_ref[rs, :] = xn
        return c

    lax.fori_loop(0, tm // ROW_CHUNK, body, 0)


def _combine(d0, d1, x, gates, yb, g_final):
    t, d = x.shape
    tm = _pick(t, (512, 256, 128))
    n = t // tm
    final = g_final is not None

    def nxt(i):
        return (jnp.minimum(i + 1, n - 1),)

    in_specs = [
        pl.BlockSpec((tm,), lambda i: (i,), memory_space=pltpu.SMEM),
        pl.BlockSpec((tm,), lambda i: (i,), memory_space=pltpu.SMEM),
        pl.BlockSpec((tm,), nxt, memory_space=pltpu.SMEM),
        pl.BlockSpec((tm,), nxt, memory_space=pltpu.SMEM),
        pl.BlockSpec((tm, d), lambda i: (i, 0)),
        pl.BlockSpec((tm, LANES), lambda i: (i, 0)),
        pl.BlockSpec(memory_space=pl.ANY),
    ]
    args = [d0, d1, d0, d1, x, gates, yb]
    if final:
        in_specs.append(pl.BlockSpec((1, d), lambda i: (0, 0)))
        args.append(g_final)
    return pl.pallas_call(
        functools.partial(_combine_kernel, final=final),
        grid=(n,),
        in_specs=in_specs,
        out_specs=pl.BlockSpec((tm, d), lambda i: (i, 0)),
        out_shape=jax.ShapeDtypeStruct((t, d), F32),
        scratch_shapes=[pltpu.VMEM((2, tm, d), F32), pltpu.VMEM((2, tm, d), F32),
                        pltpu.SemaphoreType.DMA((2, 2))],
        compiler_params=_cparams("arbitrary", disable_bounds_checks=True),
        name="moe_combine",
    )(*args)


def _moe(x, g_norm, w_route, b_route, wg, wu, wd, layer, g_final, *, n_grp, epg):
    t, d = x.shape
    n_exp = n_grp * epg
    eid, rank, gates, cnt = _router(x, g_norm, w_route, b_route, n_grp=n_grp, epg=epg)

    counts = cnt[n_grp:n_grp + n_exp, 0].astype(I32)
    pblocks = (counts + MOE_BLOCK - 1) // MOE_BLOCK
    bends = jnp.cumsum(pblocks)
    pstarts = (bends - pblocks) * MOE_BLOCK
    n_blk = (t * TOP_K) // MOE_BLOCK + n_exp
    blk = jnp.arange(n_blk, dtype=I32)
    blk_exp = jnp.minimum(jnp.sum(bends[None, :] <= blk[:, None], axis=1), n_exp - 1).astype(I32)
    n_used = bends[-1:].astype(I32)
    last_blk = jnp.where(counts > 0, bends - 1, -1)
    trail = n_used + jnp.arange(n_exp, dtype=I32)
    zero_blocks = jnp.concatenate([last_blk, jnp.where(trail < n_blk, trail, -1)]).astype(I32)

    slots = _slots(pstarts.astype(I32), eid, rank)
    d0, d1 = slots[0], slots[1]
    xb = _dispatch(zero_blocks, d0, d1, x, g_norm, n_blk * MOE_BLOCK)
    yb = _experts(blk_exp, n_used, xb, wg, wu, wd, layer)
    return _combine(d0, d1, x, gates, yb, g_final)


def _row(v):
    return v.reshape(1, -1).astype(F32)


def _trunk(x, p):
    b, s, d = x.shape
    t = b * s
    depth = p["norm_mix"].shape[0]
    xt = x.reshape(t, d)
    for i in range(depth):
        j = i // 2
        if i % 2 == 0:
            z, kt, gates = _inproj_mlstm(xt, _row(p["norm_mix"][i]), p["mlstm_w_main"][j],
                                         p["mlstm_w_kt"][j], p["mlstm_w_gate"][j])
            ng = 4 * MLSTM_HEADS
            gcol = gates.reshape(b, s, N_GATE_LANES)
            grow = jnp.swapaxes(gcol[:, :, :ng], 1, 2)
            brow, bcol = p["mlstm_b_row"][j], p["mlstm_b_col"][j]
            z3 = z.reshape(b, s, -1)
            hf = _mlstm(z3, kt, gcol, grow, brow, bcol, reverse=False)
            hn = _mlstm(z3, kt, gcol, grow, brow, bcol, reverse=True,
                        fused=(hf, _row(p["mlstm_g_hnorm"][j])))
            xt = _outproj(hn.reshape(t, -1), p["mlstm_w_out"][j], xt)
        else:
            z, ssq = _inproj_sgu(xt, _row(p["norm_mix"][i]), p["sgu_w_in"][j])
            xt = _sgu_mix(z, ssq, _row(p["sgu_g_v"][j]), p["sgu_w_s"][j], p["sgu_b_s"][j],
                          p["sgu_w_out"][j], xt)
        g_final = _row(p["norm_final"]) if i == depth - 1 else None
        xt = _moe(xt, _row(p["norm_ffn"][i]), p["moe_w_route"][i], p["moe_b_route"][i],
                  p["moe_w_gate"], p["moe_w_up"], p["moe_w_down"], i, g_final,
                  n_grp=p["n_grp"], epg=p["epg"])
    if depth == 0:
        raise ValueError("depth 0 is not supported")
    return xt.reshape(b, s, d)


def kernel(x_prompt, x_sample, norm_mix, norm_ffn, norm_final, mlstm_w_in, mlstm_b_gates, mlstm_g_hnorm, mlstm_w_out, sgu_w_in, sgu_g_v, sgu_w_s, sgu_b_s, sgu_w_out, moe_w_grp, moe_b_grp, moe_w_exp, moe_b_exp, moe_w_gate, moe_w_up, moe_w_down):
    ng = 4 * MLSTM_HEADS
    n_main = mlstm_w_in.shape[-1] - ng
    qk_dim = (n_main - 2 * mlstm_w_out.shape[-2]) // 2
    n_grp = moe_w_grp.shape[-1]
    n_exp = moe_w_exp.shape[-1]
    assert n_grp + n_exp <= LANES and ng <= N_GATE_LANES

    def pad_lanes(a, width):
        return jnp.pad(a, [(0, 0)] * (a.ndim - 1) + [(0, width - a.shape[-1])])

    p = dict(
        norm_mix=norm_mix, norm_ffn=norm_ffn, norm_final=norm_final,
        mlstm_w_main=jnp.concatenate([mlstm_w_in[..., 2 * qk_dim:n_main], mlstm_w_in[..., :qk_dim]],
                                     axis=-1).astype(BF16),
        mlstm_w_kt=jnp.swapaxes(mlstm_w_in[..., qk_dim:2 * qk_dim], -1, -2).astype(BF16),
        mlstm_w_gate=pad_lanes(mlstm_w_in[..., n_main:], N_GATE_LANES).astype(BF16),
        mlstm_b_row=pad_lanes(mlstm_b_gates, N_GATE_LANES)[:, None, :].astype(F32),
        mlstm_b_col=jnp.broadcast_to(mlstm_b_gates[:, :, None], mlstm_b_gates.shape + (MLSTM_CHUNK,)).astype(F32),
        mlstm_g_hnorm=mlstm_g_hnorm,
        mlstm_w_out=mlstm_w_out.astype(BF16),
        sgu_w_in=sgu_w_in.astype(BF16),
        sgu_g_v=sgu_g_v,
        sgu_w_s=sgu_w_s.astype(BF16),
        sgu_b_s=jnp.broadcast_to(sgu_b_s[..., None], sgu_b_s.shape + (LANES,)).astype(F32),
        sgu_w_out=sgu_w_out.astype(BF16),
        moe_w_route=pad_lanes(jnp.concatenate([moe_w_grp, moe_w_exp], axis=-1), LANES).astype(BF16),
        moe_b_route=pad_lanes(jnp.concatenate([moe_b_grp, moe_b_exp], axis=-1), LANES)[:, None, :].astype(F32),
        moe_w_gate=moe_w_gate, moe_w_up=moe_w_up, moe_w_down=moe_w_down,
        n_grp=n_grp, epg=n_exp // n_grp,
    )
    return (_trunk(x_prompt, p), _trunk(x_sample, p))
```

```python
import functools

import jax
import jax.numpy as jnp
from jax import lax
from jax.experimental import pallas as pl
from jax.experimental.pallas import tpu as pltpu

F32 = jnp.float32
BF16 = jnp.bfloat16
U32 = jnp.uint32
I32 = jnp.int32

EPS = 1e-6
MLSTM_HEADS = 4
MLSTM_CHUNK = 128
HEAD_GROUP = 4
GATE_SOFTCAP = 15.0
N_GATE_LANES = 128
SGU_CHUNK = 128
TOP_K = 2
ROW_CHUNK = 128
MOE_BLOCK = 512
DMA_UNROLL = 8
LANES = 128
VMEM_LIMIT_BYTES = 56 * 1024 * 1024


def _cparams(*sem, **kw):
    return pltpu.CompilerParams(dimension_semantics=sem, vmem_limit_bytes=VMEM_LIMIT_BYTES, **kw)


def _pick(n, prefs):
    for p in prefs:
        if n % p == 0:
            return p
    return n


def _rmsnorm_rows(x_ref, g_ref, out_ref):
    rows = x_ref.shape[0]
    g = g_ref[...]

    def body(i, c):
        r = pl.multiple_of(i * ROW_CHUNK, ROW_CHUNK)
        x = x_ref[pl.ds(r, ROW_CHUNK), :]
        ms = jnp.mean(x * x, axis=-1, keepdims=True)
        out_ref[pl.ds(r, ROW_CHUNK), :] = (x * lax.rsqrt(ms + EPS) * g).astype(out_ref.dtype)
        return c

    lax.fori_loop(0, rows // ROW_CHUNK, body, 0)


def _sigmoid(x):
    return 1.0 / (1.0 + jnp.exp(-x))


def _log_sigmoid(x):
    return jnp.minimum(x, 0.0) - jnp.log1p(jnp.exp(-jnp.abs(x)))


def _soft_cap(t):
    return GATE_SOFTCAP * jnp.tanh(t / GATE_SOFTCAP)


def _split3(a):
    hi = a.astype(BF16)
    r1 = a - hi.astype(F32)
    mid = r1.astype(BF16)
    lo = (r1 - mid.astype(F32)).astype(BF16)
    return [hi, mid, lo]


def _resident(shape):
    return pl.BlockSpec(shape, lambda i: (0,) * len(shape), pipeline_mode=pl.Buffered(1))


def _col_chunks(n):
    step = _pick(n, (1024, 512, 256, 128))
    return [(c, step) for c in range(0, n, step)]


def _inproj_mlstm_kernel(x_ref, g_ref, w_ref, wkt_ref, wg_ref, bg_ref, z_ref, kt_ref, gate_ref, xn_ref):
    _rmsnorm_rows(x_ref, g_ref, xn_ref)
    xn = xn_ref[...]
    gate_ref[...] = jnp.dot(xn, wg_ref[...], preferred_element_type=F32) + bg_ref[...]
    kt_ref[...] = lax.dot_general(wkt_ref[...], xn, (((1,), (1,)), ((), ())),
                                  preferred_element_type=F32).astype(kt_ref.dtype)
    for c, w in _col_chunks(w_ref.shape[1]):
        z_ref[:, c:c + w] = jnp.dot(xn, w_ref[:, c:c + w], preferred_element_type=F32).astype(z_ref.dtype)


def _inproj_mlstm(x, g, w, wkt, wg, bg):
    t, d = x.shape
    n = w.shape[1]
    qk = wkt.shape[0]
    tm = _pick(t, (512, 256, 128))
    return pl.pallas_call(
        _inproj_mlstm_kernel,
        grid=(t // tm,),
        in_specs=[
            pl.BlockSpec((tm, d), lambda i: (i, 0)),
            _resident((1, d)),
            _resident((d, n)),
            _resident((qk, d)),
            _resident((d, N_GATE_LANES)),
            _resident((1, N_GATE_LANES)),
        ],
        out_specs=[
            pl.BlockSpec((tm, n), lambda i: (i, 0)),
            pl.BlockSpec((qk, tm), lambda i: (0, i)),
            pl.BlockSpec((tm, N_GATE_LANES), lambda i: (i, 0)),
        ],
        out_shape=[
            jax.ShapeDtypeStruct((t, n), BF16),
            jax.ShapeDtypeStruct((qk, t), BF16),
            jax.ShapeDtypeStruct((t, N_GATE_LANES), F32),
        ],
        scratch_shapes=[pltpu.VMEM((tm, d), BF16)],
        compiler_params=_cparams("parallel"),
        name="inproj_mlstm",
    )(x, g, w, wkt, wg, bg)


def _mlstm_kernel(*refs, reverse, fuse, heads, dk, dv):
    vdim = heads * dv
    if fuse:
        n_in = 5
        z_ref, kt_ref, gate_ref, hf_ref, gh_ref = refs[:n_in]
        v_ref, v0, q_ref, q0 = z_ref, 0, z_ref, 2 * vdim
    else:
        n_in = 4
        q_ref, kt_ref, v_ref, gate_ref = refs[:n_in]
        v0, q0 = 0, 0
    out_ref = refs[n_in]
    state = refs[n_in + 1:]
    c_refs, m_refs = state[0::2], state[1::2]
    L = MLSTM_CHUNK
    ng = 4 * heads

    @pl.when(pl.program_id(1) == 0)
    def _():
        for ref in state:
            ref[...] = jnp.zeros_like(ref)

    gc = _soft_cap(gate_ref[...])
    lsc = _log_sigmoid(gc)
    row = lax.broadcasted_iota(I32, (L, L), 0)
    col = lax.broadcasted_iota(I32, (L, L), 1)
    mask_t = col >= row if reverse else col <= row
    incl_t = jnp.where(mask_t, 1.0, 0.0).astype(BF16)
    cc = jnp.dot(incl_t, jnp.concatenate(_split3(lsc), axis=1), preferred_element_type=F32)
    cum_cols = cc[:, :LANES] + cc[:, LANES:2 * LANES] + cc[:, 2 * LANES:]
    gr = gc.T[:ng, :]
    cum_rows = cum_cols.T[:ng, :]
    end = 0 if reverse else L - 1
    scale = dk ** -0.5
    reps = dv // LANES

    def lanes(a, n):
        return jnp.concatenate([a] * n, axis=1) if n > 1 else a

    def first(h):
        ci = (2 * heads if reverse else 0) + h
        cf = ci + heads
        i_col = jnp.broadcast_to(gc[:, ci:ci + 1], (L, LANES))
        b_col = jnp.broadcast_to(cum_cols[:, cf:cf + 1], (L, LANES))
        i_row = gr[ci:ci + 1, :]
        b_row = cum_rows[cf:cf + 1, :]
        g = b_col[end:end + 1, :]
        m_prev = m_refs[h][...]
        a_col = g - b_col + i_col
        m_new = jnp.maximum(g + m_prev, jnp.max(a_col, axis=0, keepdims=True))
        decay = jnp.exp(g + m_prev - m_new)
        w_row = jnp.exp(g - b_row + i_row - m_new)

        q = q_ref[:, q0 + h * dk:q0 + (h + 1) * dk]
        kt = kt_ref[h * dk:(h + 1) * dk, :]
        qs_b = (q.astype(F32) * scale).astype(BF16)
        cx_old = c_refs[h][...]
        num_x = jnp.dot(qs_b, cx_old.astype(BF16), preferred_element_type=F32)
        qk = jnp.dot(qs_b, kt, preferred_element_type=F32)

        log_inter = b_col + m_prev
        dlog = jnp.where(mask_t, b_col - b_row + i_row, -jnp.inf)
        m_t = jnp.maximum(log_inter, jnp.max(dlog, axis=1, keepdims=True))
        return m_new, decay, w_row, kt, cx_old, num_x, qk, log_inter, dlog, m_t

    ones = jnp.ones((L, LANES), BF16)

    def second(h, staged):
        m_new, decay, w_row, kt, cx_old, num_x, qk, log_inter, dlog, m_t = staged
        vx = jnp.concatenate([v_ref[:, v0 + h * dv:v0 + (h + 1) * dv], ones], axis=1)
        s = qk * jnp.exp(dlog - m_t)
        inter = jnp.exp(log_inter - m_t)
        sv = jnp.dot(s.astype(BF16), vx, preferred_element_type=F32)
        num = lanes(inter, reps) * num_x[:, :dv] + sv[:, :dv]
        den = inter * num_x[:, dv:] + sv[:, dv:]
        hh = num * lanes(1.0 / jnp.maximum(jnp.abs(den), jnp.exp(-m_t)), reps)

        kwt = (kt.astype(F32) * w_row).astype(BF16)
        c_refs[h][...] = lanes(decay, reps + 1) * cx_old + jnp.dot(kwt, vx, preferred_element_type=F32)
        m_refs[h][...] = m_new

        hs = slice(h * dv, (h + 1) * dv)
        if fuse:
            tot = hf_ref[:, hs] + hh
            ms = jnp.mean(tot * tot, axis=-1, keepdims=True)
            o = z_ref[:, vdim + h * dv:vdim + (h + 1) * dv].astype(F32)
            gated = tot * lax.rsqrt(ms + EPS) * gh_ref[:, hs] * _sigmoid(o)
            out_ref[:, hs] = gated.astype(out_ref.dtype)
        else:
            out_ref[:, hs] = hh

    for h0 in range(0, heads, HEAD_GROUP):
        group = range(h0, min(h0 + HEAD_GROUP, heads))
        staged = [first(h) for h in group]
        for h, st in zip(group, staged):
            second(h, st)


def _mlstm(z, kt, gates, *, reverse, fused=None):
    b, s, width = z.shape
    heads = MLSTM_HEADS
    qk = kt.shape[0]
    vdim = (width - qk) // 2
    dk, dv = qk // heads, vdim // heads
    nc = s // MLSTM_CHUNK
    L = MLSTM_CHUNK
    assert L == N_GATE_LANES and dv % LANES == 0 and (2 * vdim) % qk == 0

    def ch(c):
        return nc - 1 - c if reverse else c

    kt_spec = pl.BlockSpec((qk, L), lambda bi, c: (0, bi * nc + ch(c)))
    gate_spec = pl.BlockSpec((None, L, N_GATE_LANES), lambda bi, c: (bi, ch(c), 0))
    if fused is None:
        in_specs = [
            pl.BlockSpec((None, L, qk), lambda bi, c: (bi, ch(c), (2 * vdim) // qk)),
            kt_spec,
            pl.BlockSpec((None, L, vdim), lambda bi, c: (bi, ch(c), 0)),
            gate_spec,
        ]
        args = [z, kt, z, gates]
    else:
        hf, gh = fused
        in_specs = [
            pl.BlockSpec((None, L, width), lambda bi, c: (bi, ch(c), 0)),
            kt_spec,
            gate_spec,
            pl.BlockSpec((None, L, vdim), lambda bi, c: (bi, ch(c), 0)),
            pl.BlockSpec((1, vdim), lambda bi, c: (0, 0)),
        ]
        args = [z, kt, gates, hf, gh]
    return pl.pallas_call(
        functools.partial(_mlstm_kernel, reverse=reverse, fuse=fused is not None,
                          heads=heads, dk=dk, dv=dv),
        grid=(b, nc),
        in_specs=in_specs,
        out_specs=pl.BlockSpec((None, L, vdim), lambda bi, c: (bi, ch(c), 0)),
        out_shape=jax.ShapeDtypeStruct((b, s, vdim), BF16 if fused is not None else F32),
        scratch_shapes=[pltpu.VMEM((dk, dv + LANES), F32), pltpu.VMEM((1, LANES), F32)] * heads,
        compiler_params=_cparams("parallel", "arbitrary"),
        name="mlstm_bwd" if reverse else "mlstm_fwd",
    )(*args)


def _outproj_kernel(h_ref, w_ref, x_ref, o_ref):
    h = h_ref[...]
    for c, w in _col_chunks(w_ref.shape[1]):
        o_ref[:, c:c + w] = x_ref[:, c:c + w] + jnp.dot(h, w_ref[:, c:c + w], preferred_element_type=F32)


def _outproj(h, w, x):
    t, kdim = h.shape
    n = w.shape[1]
    tm = _pick(t, (512, 256, 128))
    return pl.pallas_call(
        _outproj_kernel,
        grid=(t // tm,),
        in_specs=[
            pl.BlockSpec((tm, kdim), lambda i: (i, 0)),
            _resident((kdim, n)),
            pl.BlockSpec((tm, n), lambda i: (i, 0)),
        ],
        out_specs=pl.BlockSpec((tm, n), lambda i: (i, 0)),
        out_shape=jax.ShapeDtypeStruct((t, n), F32),
        compiler_params=_cparams("parallel"),
        name="outproj",
    )(h, w, x)


def _inproj_sgu_kernel(x_ref, g_ref, w_ref, z_ref, ssq_ref, xn_ref):
    _rmsnorm_rows(x_ref, g_ref, xn_ref)
    xn = xn_ref[...]
    n = w_ref.shape[1]
    part = None
    for c, w in _col_chunks(n // 2) + [(n // 2 + c, w) for c, w in _col_chunks(n // 2)]:
        z = jax.nn.gelu(jnp.dot(xn, w_ref[:, c:c + w], preferred_element_type=F32))
        z_ref[:, c:c + w] = z.astype(z_ref.dtype)
        if c >= n // 2:
            z2 = z * z
            for l in range(0, w, LANES):
                part = z2[:, l:l + LANES] if part is None else part + z2[:, l:l + LANES]
    ssq_ref[...] = part


def _inproj_sgu(x, g, w):
    t, d = x.shape
    n = w.shape[1]
    tm = _pick(t, (512, 256, 128))
    return pl.pallas_call(
        _inproj_sgu_kernel,
        grid=(t // tm,),
        in_specs=[
            pl.BlockSpec((tm, d), lambda i: (i, 0)),
            _resident((1, d)),
            _resident((d, n)),
        ],
        out_specs=[
            pl.BlockSpec((tm, n), lambda i: (i, 0)),
            pl.BlockSpec((tm, LANES), lambda i: (i, 0)),
        ],
        out_shape=[
            jax.ShapeDtypeStruct((t, n), BF16),
            jax.ShapeDtypeStruct((t, LANES), F32),
        ],
        scratch_shapes=[pltpu.VMEM((tm, d), BF16)],
        compiler_params=_cparams("parallel"),
        name="inproj_sgu",
    )(x, g, w)


def _sgu_mix_kernel(u_ref, v_ref, ssq_ref, gv_ref, ws_ref, bs_ref, wo_ref, x_ref, o_ref, gated_ref,
                    *, groups, gd):
    rows = u_ref.shape[0]
    width = v_ref.shape[1]
    gv = gv_ref[...]
    for c in range(rows // SGU_CHUNK):
        rs = slice(c * SGU_CHUNK, (c + 1) * SGU_CHUNK)
        ms = jnp.sum(ssq_ref[rs, :], axis=-1, keepdims=True) / width
        vn = (v_ref[rs, :].astype(F32) * lax.rsqrt(ms + EPS) * gv).astype(BF16)
        for gi in range(groups):
            cs = slice(gi * gd, (gi + 1) * gd)
            mixed = jnp.dot(ws_ref[gi], vn[:, cs], preferred_element_type=F32)
            bias = bs_ref[gi]
            bias = jnp.concatenate([bias] * (gd // LANES), axis=1) if gd > LANES else bias
            gated_ref[rs, cs] = (u_ref[rs, cs].astype(F32) * (mixed + bias)).astype(BF16)
    gated = gated_ref[...]
    for c, w in _col_chunks(wo_ref.shape[1]):
        o_ref[:, c:c + w] = x_ref[:, c:c + w] + jnp.dot(gated, wo_ref[:, c:c + w], preferred_element_type=F32)


def _sgu_mix(z, ssq, gv, ws, bs, wo, x):
    t, d = x.shape
    width = z.shape[1] // 2
    groups = ws.shape[0]
    tm = _pick(t, (512, 256, 128))
    return pl.pallas_call(
        functools.partial(_sgu_mix_kernel, groups=groups, gd=width // groups),
        grid=(t // tm,),
        in_specs=[
            pl.BlockSpec((tm, width), lambda i: (i, 0)),
            pl.BlockSpec((tm, width), lambda i: (i, 1)),
            pl.BlockSpec((tm, LANES), lambda i: (i, 0)),
            _resident((1, width)),
            _resident((groups, SGU_CHUNK, SGU_CHUNK)),
            _resident((groups, SGU_CHUNK, LANES)),
            _resident((width, d)),
            pl.BlockSpec((tm, d), lambda i: (i, 0)),
        ],
        out_specs=pl.BlockSpec((tm, d), lambda i: (i, 0)),
        out_shape=jax.ShapeDtypeStruct((t, d), F32),
        scratch_shapes=[pltpu.VMEM((tm, width), BF16)],
        compiler_params=_cparams("parallel"),
        name="sgu_mix",
    )(z, z, ssq, gv, ws, bs, wo, x)


def _router_kernel(x_ref, g_ref, w_ref, b_ref, eid_ref, rank_ref, gate_ref, cnt_ref, hn_ref, carry_ref,
                   *, n_grp, epg, rows):
    tm = x_ref.shape[0]

    @pl.when(pl.program_id(0) == 0)
    def _():
        carry_ref[...] = jnp.zeros_like(carry_ref)

    _rmsnorm_rows(x_ref, g_ref, hn_ref)
    logits = jnp.dot(hn_ref[...], w_ref[...], preferred_element_type=F32) + b_ref[...]
    lt = logits.T[:rows, :]
    ridx = lax.broadcasted_iota(I32, (rows, tm), 0).astype(F32)
    big = float(rows)

    gl = jnp.where(ridx < n_grp, lt, -jnp.inf)
    gmax = jnp.max(gl, axis=0, keepdims=True)
    grp = jnp.min(jnp.where(gl == gmax, ridx, big), axis=0, keepdims=True)
    grp_p = 1.0 / jnp.sum(jnp.exp(gl - gmax), axis=0, keepdims=True)

    lo = n_grp + grp * epg
    el = jnp.where(ridx >= lo, jnp.where(ridx < lo + epg, lt, -jnp.inf), -jnp.inf)
    emax = jnp.max(el, axis=0, keepdims=True)
    ee = jnp.exp(el - emax)
    p = ee / jnp.sum(ee, axis=0, keepdims=True)
    pm = jnp.where(ridx >= lo, jnp.where(ridx < lo + epg, p, -1.0), -1.0)
    p1 = jnp.max(pm, axis=0, keepdims=True)
    i1 = jnp.min(jnp.where(pm == p1, ridx, big), axis=0, keepdims=True)
    pm2 = jnp.where(ridx == i1, -1.0, pm)
    p2 = jnp.max(pm2, axis=0, keepdims=True)
    i2 = jnp.min(jnp.where(pm2 == p2, ridx, big), axis=0, keepdims=True)
    psum = p1 + p2
    g1 = grp_p * (p1 / psum)
    g2 = grp_p * (p2 / psum)

    hit1 = ridx == i1
    hit2 = ridx == i2
    onehot = jnp.where(hit1, 1.0, jnp.where(hit2, 1.0, 0.0))
    r = lax.broadcasted_iota(I32, (tm, tm), 0)
    c = lax.broadcasted_iota(I32, (tm, tm), 1)
    earlier = jnp.where(r < c, 1.0, 0.0).astype(BF16)
    carry = carry_ref[:, 0:1]
    before = jnp.dot(onehot.astype(BF16), earlier, preferred_element_type=F32) + carry
    rank1 = jnp.sum(jnp.where(hit1, before, 0.0), axis=0, keepdims=True)
    rank2 = jnp.sum(jnp.where(hit2, before, 0.0), axis=0, keepdims=True)
    carry = carry + jnp.sum(onehot, axis=1, keepdims=True)
    carry_ref[...] = jnp.broadcast_to(carry, carry_ref.shape)
    cnt_ref[...] = jnp.broadcast_to(carry, cnt_ref.shape)

    sub = lax.broadcasted_iota(I32, (8, tm), 0)
    eid_ref[...] = jnp.where(sub == 0, i1 - n_grp, jnp.where(sub == 1, i2 - n_grp, 0.0)).astype(I32)
    rank_ref[...] = jnp.where(sub == 0, rank1, jnp.where(sub == 1, rank2, 0.0)).astype(I32)
    sub_l = lax.broadcasted_iota(I32, (LANES, tm), 0)
    gate_ref[...] = jnp.where(sub_l == 0, g1, jnp.where(sub_l == 1, g2, 0.0)).T


def _router(x, g, w, b, *, n_grp, epg):
    t, d = x.shape
    tm = _pick(t, (512, 256, 128))
    rows = -(-(n_grp + n_grp * epg) // 8) * 8
    return pl.pallas_call(
        functools.partial(_router_kernel, n_grp=n_grp, epg=epg, rows=rows),
        grid=(t // tm,),
        in_specs=[
            pl.BlockSpec((tm, d), lambda i: (i, 0)),
            pl.BlockSpec((1, d), lambda i: (0, 0)),
            pl.BlockSpec((d, LANES), lambda i: (0, 0)),
            pl.BlockSpec((1, LANES), lambda i: (0, 0)),
        ],
        out_specs=[
            pl.BlockSpec((8, tm), lambda i: (0, i)),
            pl.BlockSpec((8, tm), lambda i: (0, i)),
            pl.BlockSpec((tm, LANES), lambda i: (i, 0)),
            pl.BlockSpec((rows, LANES), lambda i: (0, 0)),
        ],
        out_shape=[
            jax.ShapeDtypeStruct((8, t), I32),
            jax.ShapeDtypeStruct((8, t), I32),
            jax.ShapeDtypeStruct((t, LANES), F32),
            jax.ShapeDtypeStruct((rows, LANES), F32),
        ],
        scratch_shapes=[pltpu.VMEM((tm, d), BF16), pltpu.VMEM((rows, LANES), F32)],
        compiler_params=_cparams("arbitrary"),
        name="router",
    )(x, g, w, b)


def _slots_kernel(ps_ref, eid_ref, rank_ref, slot_ref, *, n_exp):
    eid = eid_ref[...]
    base = jnp.zeros_like(eid)
    for e in range(n_exp):
        base = jnp.where(eid == e, ps_ref[e], base)
    slot_ref[...] = base + rank_ref[...]


def _slots(pstarts, eid, rank):
    rows, t = eid.shape
    tm = _pick(t, (4096, 2048, 1024, 512, 256, 128))
    return pl.pallas_call(
        functools.partial(_slots_kernel, n_exp=pstarts.shape[0]),
        grid_spec=pltpu.PrefetchScalarGridSpec(
            num_scalar_prefetch=1,
            grid=(t // tm,),
            in_specs=[
                pl.BlockSpec((rows, tm), lambda i, ps: (0, i)),
                pl.BlockSpec((rows, tm), lambda i, ps: (0, i)),
            ],
            out_specs=pl.BlockSpec((rows, tm), lambda i, ps: (0, i)),
        ),
        out_shape=jax.ShapeDtypeStruct((rows, t), I32),
        compiler_params=_cparams("parallel"),
        name="moe_slots",
    )(pstarts, eid, rank)


def _dispatch_kernel(zb_ref, d0_ref, d1_ref, x_ref, g_ref, xb_ref, pk_ref, zero_ref, sem, zsem):
    i = pl.program_id(0)
    last = pl.num_programs(0) - 1
    tm, d = x_ref.shape
    half = d // 2
    rt = half // LANES
    slot = i % 2
    g = g_ref[...]

    @pl.when(i == 0)
    def _():
        zero_ref[...] = jnp.zeros_like(zero_ref)

        def zero_copy(j):
            lines = MOE_BLOCK * rt
            start = pl.multiple_of(jnp.maximum(zb_ref[j], 0) * lines, lines)
            return pltpu.make_async_copy(zero_ref, xb_ref.at[pl.ds(start, lines)], zsem.at[0])

        def zstart(j, c):
            @pl.when(zb_ref[j] >= 0)
            def _():
                zero_copy(j).start()
            return c

        def zwait(j, c):
            @pl.when(zb_ref[j] >= 0)
            def _():
                zero_copy(j).wait()
            return c

        lax.fori_loop(0, zb_ref.shape[0], zstart, 0)
        lax.fori_loop(0, zb_ref.shape[0], zwait, 0)

    def pack(j, c):
        r = pl.multiple_of(j * ROW_CHUNK, ROW_CHUNK)
        x = x_ref[pl.ds(r, ROW_CHUNK), :]
        ms = jnp.mean(x * x, axis=-1, keepdims=True)
        hn = (x * lax.rsqrt(ms + EPS) * g).astype(BF16).astype(F32)
        bits = lax.bitcast_convert_type(hn, U32)
        lo = lax.shift_right_logical(bits[:, :half], jnp.uint32(16))
        hi = bits[:, half:] & jnp.uint32(0xFFFF0000)
        words = hi | lo
        for s in range(rt):
            pk_ref[slot, pl.ds(r * rt + s, ROW_CHUNK, stride=rt), :] = words[:, s * LANES:(s + 1) * LANES]
        return c

    lax.fori_loop(0, tm // ROW_CHUNK, pack, 0)

    def row_copy(r, dest, prio):
        src = pk_ref.at[slot, pl.ds(pl.multiple_of(r * rt, rt), rt)]
        dst = xb_ref.at[pl.ds(pl.multiple_of(dest * rt, rt), rt)]
        pltpu.make_async_copy(src, dst, sem.at[slot]).start(priority=prio)

    def issue(j, c):
        for u in range(DMA_UNROLL):
            r = j * DMA_UNROLL + u
            row_copy(r, d0_ref[r], u % 2)
            row_copy(r, d1_ref[r], (u + 1) % 2)
        return c

    lax.fori_loop(0, tm // DMA_UNROLL, issue, 0)

    def drain(s):
        for _ in range(TOP_K):
            pltpu.make_async_copy(pk_ref.at[s], xb_ref.at[pl.ds(0, tm * rt)], sem.at[s]).wait()

    @pl.when(i > 0)
    def _():
        drain(1 - slot)

    @pl.when(i == last)
    def _():
        drain(slot)


def _dispatch(zero_blocks, d0, d1, x, g, n_rows):
    t, d = x.shape
    tm = _pick(t, (512, 256, 128))
    rt = d // 2 // LANES
    return pl.pallas_call(
        _dispatch_kernel,
        grid_spec=pltpu.PrefetchScalarGridSpec(
            num_scalar_prefetch=1,
            grid=(t // tm,),
            in_specs=[
                pl.BlockSpec((tm,), lambda i, zb: (i,), memory_space=pltpu.SMEM),
                pl.BlockSpec((tm,), lambda i, zb: (i,), memory_space=pltpu.SMEM),
                pl.BlockSpec((tm, d), lambda i, zb: (i, 0)),
                pl.BlockSpec((1, d), lambda i, zb: (0, 0)),
            ],
            out_specs=pl.BlockSpec(memory_space=pl.ANY),
            scratch_shapes=[pltpu.VMEM((2, tm * rt, LANES), U32), pltpu.VMEM((MOE_BLOCK * rt, LANES), U32),
                            pltpu.SemaphoreType.DMA((2,)), pltpu.SemaphoreType.DMA((1,))],
        ),
        out_shape=jax.ShapeDtypeStruct((n_rows * rt, LANES), U32),
        compiler_params=_cparams("arbitrary", disable_bounds_checks=True),
        name="moe_dispatch",
    )(zero_blocks, d0, d1, x, g)


def _cast_rows(src_ref, dst_ref):
    rows = src_ref.shape[0]
    step = _pick(rows, (256, 128))

    def body(j, c):
        r = pl.multiple_of(j * step, step)
        dst_ref[pl.ds(r, step), :] = src_ref[pl.ds(r, step), :].astype(dst_ref.dtype)
        return c

    lax.fori_loop(0, rows // step, body, 0)


def _experts_kernel(be_ref, nu_ref, xb_ref, wg_ref, wu_ref, wd_ref, yb_ref, wgb_ref, wub_ref, wdb_ref):
    s = pl.program_id(0)
    nu = nu_ref[0]
    blk = s - 1
    half = wgb_ref.shape[0] // 2
    rt_in = half // LANES

    @pl.when(jnp.logical_and(s >= 1, blk < nu))
    def _():
        w = jnp.concatenate([xb_ref[pl.ds(t, MOE_BLOCK, stride=rt_in), :] for t in range(rt_in)], axis=1)
        x_lo = lax.bitcast_convert_type(lax.shift_left(w, jnp.uint32(16)), F32).astype(BF16)
        x_hi = lax.bitcast_convert_type(w & jnp.uint32(0xFFFF0000), F32).astype(BF16)
        h1 = (jnp.dot(x_lo, wgb_ref[:half, :], preferred_element_type=F32)
              + jnp.dot(x_hi, wgb_ref[half:, :], preferred_element_type=F32))
        h2 = (jnp.dot(x_lo, wub_ref[:half, :], preferred_element_type=F32)
              + jnp.dot(x_hi, wub_ref[half:, :], preferred_element_type=F32))
        hid = ((h1 * _sigmoid(h1)) * h2).astype(BF16)
        for c, width in _col_chunks(wdb_ref.shape[1]):
            yb_ref[:, c:c + width] = jnp.dot(hid, wdb_ref[:, c:c + width], preferred_element_type=F32)

    @pl.when(jnp.logical_and(s >= 1, blk >= nu))
    def _():
        yb_ref[...] = jnp.zeros_like(yb_ref)

    last_blk = be_ref.shape[0] - 1
    changed = be_ref[jnp.minimum(s, last_blk)] != be_ref[jnp.clip(s - 1, 0, last_blk)]

    @pl.when(jnp.logical_and(s < nu, jnp.logical_or(s == 0, changed)))
    def _():
        _cast_rows(wg_ref, wgb_ref)
        _cast_rows(wu_ref, wub_ref)
        _cast_rows(wd_ref, wdb_ref)


def _experts(blk_exp, n_used, xb, wg, wu, wd, layer):
    d, de = wg.shape[-2:]
    rt_in = d // 2 // LANES
    n_blk = xb.shape[0] // (MOE_BLOCK * rt_in)

    def weight(s, be, nu):
        return (layer, be[jnp.minimum(s, nu[0] - 1)], 0, 0)

    return pl.pallas_call(
        _experts_kernel,
        grid_spec=pltpu.PrefetchScalarGridSpec(
            num_scalar_prefetch=2,
            grid=(n_blk + 1,),
            in_specs=[
                pl.BlockSpec((MOE_BLOCK * rt_in, LANES),
                             lambda s, be, nu: (jnp.clip(s - 1, 0, nu[0] - 1), 0)),
                pl.BlockSpec((None, None, d, de), weight),
                pl.BlockSpec((None, None, d, de), weight),
                pl.BlockSpec((None, None, de, d), weight),
            ],
            out_specs=pl.BlockSpec((MOE_BLOCK, d), lambda s, be, nu: (jnp.maximum(s - 1, 0), 0)),
            scratch_shapes=[pltpu.VMEM((d, de), BF16), pltpu.VMEM((d, de), BF16),
                            pltpu.VMEM((de, d), BF16)],
        ),
        out_shape=jax.ShapeDtypeStruct((n_blk * MOE_BLOCK, d), F32),
        compiler_params=_cparams("arbitrary"),
        name="moe_experts",
    )(blk_exp, n_used, xb, wg, wu, wd)


def _combine_kernel(*refs, final):
    if final:
        d0_ref, d1_ref, x_ref, gate_ref, yb_ref, gf_ref, o_ref, a_ref, b_ref, sem = refs
    else:
        d0_ref, d1_ref, x_ref, gate_ref, yb_ref, o_ref, a_ref, b_ref, sem = refs
    i = pl.program_id(0)
    n_tiles = pl.num_programs(0) - 1
    tm = x_ref.shape[0]
    slot = i % 2

    @pl.when(i < n_tiles)
    def _():
        def issue(j, c):
            for u in range(DMA_UNROLL):
                r = j * DMA_UNROLL + u
                pltpu.make_async_copy(yb_ref.at[pl.ds(d0_ref[r], 1)], a_ref.at[slot, pl.ds(r, 1)],
                                      sem.at[0, slot]).start(priority=u % 2)
                pltpu.make_async_copy(yb_ref.at[pl.ds(d1_ref[r], 1)], b_ref.at[slot, pl.ds(r, 1)],
                                      sem.at[1, slot]).start(priority=(u + 1) % 2)
            return c

        lax.fori_loop(0, tm // DMA_UNROLL, issue, 0)

    @pl.when(i > 0)
    def _():
        prev = 1 - slot
        pltpu.make_async_copy(yb_ref.at[pl.ds(0, tm)], a_ref.at[prev], sem.at[0, prev]).wait()
        pltpu.make_async_copy(yb_ref.at[pl.ds(0, tm)], b_ref.at[prev], sem.at[1, prev]).wait()

        def body(j, c):
            r = pl.multiple_of(j * ROW_CHUNK, ROW_CHUNK)
            rs = pl.ds(r, ROW_CHUNK)
            gates = gate_ref[rs, :]
            y = gates[:, 0:1] * a_ref[prev, rs, :] + gates[:, 1:2] * b_ref[prev, rs, :]
            xn = x_ref[rs, :] + y
            if final:
                ms = jnp.mean(xn * xn, axis=-1, keepdims=True)
                xn = xn * lax.rsqrt(ms + EPS) * gf_ref[...]
            o_ref[rs, :] = xn
            return c

        lax.fori_loop(0, tm // ROW_CHUNK, body, 0)


def _combine(d0, d1, x, gates, yb, g_final):
    t, d = x.shape
    tm = _pick(t, (256, 128))
    n = t // tm
    final = g_final is not None

    def cur(i):
        return (jnp.minimum(i, n - 1),)

    def behind(i):
        return (jnp.maximum(i - 1, 0), 0)

    in_specs = [
        pl.BlockSpec((tm,), cur, memory_space=pltpu.SMEM),
        pl.BlockSpec((tm,), cur, memory_space=pltpu.SMEM),
        pl.BlockSpec((tm, d), behind),
        pl.BlockSpec((tm, LANES), behind),
        pl.BlockSpec(memory_space=pl.ANY),
    ]
    args = [d0, d1, x, gates, yb]
    if final:
        in_specs.append(pl.BlockSpec((1, d), lambda i: (0, 0)))
        args.append(g_final)
    return pl.pallas_call(
        functools.partial(_combine_kernel, final=final),
        grid=(n + 1,),
        in_specs=in_specs,
        out_specs=pl.BlockSpec((tm, d), behind),
        out_shape=jax.ShapeDtypeStruct((t, d), F32),
        scratch_shapes=[pltpu.VMEM((2, tm, d), F32), pltpu.VMEM((2, tm, d), F32),
                        pltpu.SemaphoreType.DMA((2, 2))],
        compiler_params=_cparams("arbitrary", disable_bounds_checks=True),
        name="moe_combine",
    )(*args)


def _moe(x, g_norm, w_route, b_route, wg, wu, wd, layer, g_final, *, n_grp, epg):
    t, d = x.shape
    n_exp = n_grp * epg
    eid, rank, gates, cnt = _router(x, g_norm, w_route, b_route, n_grp=n_grp, epg=epg)

    counts = cnt[n_grp:n_grp + n_exp, 0].astype(I32)
    pblocks = (counts + MOE_BLOCK - 1) // MOE_BLOCK
    bends = jnp.cumsum(pblocks)
    pstarts = (bends - pblocks) * MOE_BLOCK
    n_blk = (t * TOP_K) // MOE_BLOCK + n_exp
    blk = jnp.arange(n_blk, dtype=I32)
    blk_exp = jnp.minimum(jnp.sum(bends[None, :] <= blk[:, None], axis=1), n_exp - 1).astype(I32)
    n_used = bends[-1:].astype(I32)
    last_blk = jnp.where(counts > 0, bends - 1, -1)
    trail = n_used + jnp.arange(n_exp, dtype=I32)
    zero_blocks = jnp.concatenate([last_blk, jnp.where(trail < n_blk, trail, -1)]).astype(I32)

    slots = _slots(pstarts.astype(I32), eid, rank)
    d0, d1 = slots[0], slots[1]
    xb = _dispatch(zero_blocks, d0, d1, x, g_norm, n_blk * MOE_BLOCK)
    yb = _experts(blk_exp, n_used, xb, wg, wu, wd, layer)
    return _combine(d0, d1, x, gates, yb, g_final)


def _row(v):
    return v.reshape(1, -1).astype(F32)


def _trunk(x, p):
    b, s, d = x.shape
    t = b * s
    depth = p["norm_mix"].shape[0]
    xt = x.reshape(t, d)
    for i in range(depth):
        j = i // 2
        if i % 2 == 0:
            z, kt, gates = _inproj_mlstm(xt, _row(p["norm_mix"][i]), p["mlstm_w_main"][j],
                                         p["mlstm_w_kt"][j], p["mlstm_w_gate"][j], p["mlstm_b_row"][j])
            g3 = gates.reshape(b, s, N_GATE_LANES)
            z3 = z.reshape(b, s, -1)
            hf = _mlstm(z3, kt, g3, reverse=False)
            hn = _mlstm(z3, kt, g3, reverse=True, fused=(hf, _row(p["mlstm_g_hnorm"][j])))
            xt = _outproj(hn.reshape(t, -1), p["mlstm_w_out"][j], xt)
        else:
            z, ssq = _inproj_sgu(xt, _row(p["norm_mix"][i]), p["sgu_w_in"][j])
            xt = _sgu_mix(z, ssq, _row(p["sgu_g_v"][j]), p["sgu_w_s"][j], p["sgu_b_s"][j],
                          p["sgu_w_out"][j], xt)
        g_final = _row(p["norm_final"]) if i == depth - 1 else None
        xt = _moe(xt, _row(p["norm_ffn"][i]), p["moe_w_route"][i], p["moe_b_route"][i],
                  p["moe_w_gate"], p["moe_w_up"], p["moe_w_down"], i, g_final,
                  n_grp=p["n_grp"], epg=p["epg"])
    if depth == 0:
        raise ValueError("depth 0 is not supported")
    return xt.reshape(b, s, d)


def kernel(x_prompt, x_sample, norm_mix, norm_ffn, norm_final, mlstm_w_in, mlstm_b_gates, mlstm_g_hnorm, mlstm_w_out, sgu_w_in, sgu_g_v, sgu_w_s, sgu_b_s, sgu_w_out, moe_w_grp, moe_b_grp, moe_w_exp, moe_b_exp, moe_w_gate, moe_w_up, moe_w_down):
    ng = 4 * MLSTM_HEADS
    n_main = mlstm_w_in.shape[-1] - ng
    qk_dim = (n_main - 2 * mlstm_w_out.shape[-2]) // 2
    n_grp = moe_w_grp.shape[-1]
    n_exp = moe_w_exp.shape[-1]
    assert n_grp + n_exp <= LANES and ng <= N_GATE_LANES

    def pad_lanes(a, width):
        return jnp.pad(a, [(0, 0)] * (a.ndim - 1) + [(0, width - a.shape[-1])])

    p = dict(
        norm_mix=norm_mix, norm_ffn=norm_ffn, norm_final=norm_final,
        mlstm_w_main=jnp.concatenate([mlstm_w_in[..., 2 * qk_dim:n_main], mlstm_w_in[..., :qk_dim]],
                                     axis=-1).astype(BF16),
        mlstm_w_kt=jnp.swapaxes(mlstm_w_in[..., qk_dim:2 * qk_dim], -1, -2).astype(BF16),
        mlstm_w_gate=pad_lanes(mlstm_w_in[..., n_main:], N_GATE_LANES).astype(BF16),
        mlstm_b_row=pad_lanes(mlstm_b_gates, N_GATE_LANES)[:, None, :].astype(F32),
        mlstm_g_hnorm=mlstm_g_hnorm,
        mlstm_w_out=mlstm_w_out.astype(BF16),
        sgu_w_in=sgu_w_in.astype(BF16),
        sgu_g_v=sgu_g_v,
        sgu_w_s=sgu_w_s.astype(BF16),
        sgu_b_s=jnp.broadcast_to(sgu_b_s[..., None], sgu_b_s.shape + (LANES,)).astype(F32),
        sgu_w_out=sgu_w_out.astype(BF16),
        moe_w_route=pad_lanes(jnp.concatenate([moe_w_grp, moe_w_exp], axis=-1), LANES).astype(BF16),
        moe_b_route=pad_lanes(jnp.concatenate([moe_b_grp, moe_b_exp], axis=-1), LANES)[:, None, :].astype(F32),
        moe_w_gate=moe_w_gate, moe_w_up=moe_w_up, moe_w_down=moe_w_down,
        n_grp=n_grp, epg=n_exp // n_grp,
    )
    return (_trunk(x_prompt, p), _trunk(x_sample, p))
```

```python
import functools

import jax
import jax.numpy as jnp
from jax import lax
from jax.experimental import pallas as pl
from jax.experimental.pallas import tpu as pltpu

F32 = jnp.float32
BF16 = jnp.bfloat16
U32 = jnp.uint32
I32 = jnp.int32

EPS = 1e-6
MLSTM_HEADS = 4
MLSTM_CHUNK = 128
HEAD_GROUP = 4
MLSTM_CHUNKS_PER_STEP = 4
GATE_SOFTCAP = 15.0
N_GATE_LANES = 128
SGU_CHUNK = 128
TOP_K = 2
ROW_CHUNK = 128
MOE_BLOCK = 512
DMA_UNROLL = 8
LANES = 128
VMEM_LIMIT_BYTES = 56 * 1024 * 1024


def _cparams(*sem, **kw):
    return pltpu.CompilerParams(dimension_semantics=sem, vmem_limit_bytes=VMEM_LIMIT_BYTES, **kw)


def _pick(n, prefs):
    for p in prefs:
        if n % p == 0:
            return p
    return n


def _rmsnorm_rows(x_ref, g_ref, out_ref):
    rows = x_ref.shape[0]
    g = g_ref[...]

    def body(i, c):
        r = pl.multiple_of(i * ROW_CHUNK, ROW_CHUNK)
        x = x_ref[pl.ds(r, ROW_CHUNK), :]
        ms = jnp.mean(x * x, axis=-1, keepdims=True)
        out_ref[pl.ds(r, ROW_CHUNK), :] = (x * lax.rsqrt(ms + EPS) * g).astype(out_ref.dtype)
        return c

    lax.fori_loop(0, rows // ROW_CHUNK, body, 0)


def _sigmoid(x):
    return 1.0 / (1.0 + jnp.exp(-x))


def _log_sigmoid(x):
    return jnp.minimum(x, 0.0) - jnp.log1p(jnp.exp(-jnp.abs(x)))


def _soft_cap(t):
    return GATE_SOFTCAP * jnp.tanh(t / GATE_SOFTCAP)


def _split3(a):
    hi = a.astype(BF16)
    r1 = a - hi.astype(F32)
    mid = r1.astype(BF16)
    lo = (r1 - mid.astype(F32)).astype(BF16)
    return [hi, mid, lo]


def _resident(shape):
    return pl.BlockSpec(shape, lambda i: (0,) * len(shape), pipeline_mode=pl.Buffered(1))


def _col_chunks(n):
    step = _pick(n, (1024, 512, 256, 128))
    return [(c, step) for c in range(0, n, step)]


def _inproj_mlstm_kernel(x_ref, g_ref, w_ref, wkt_ref, wg_ref, bg_ref, z_ref, kt_ref, gate_ref, xn_ref):
    _rmsnorm_rows(x_ref, g_ref, xn_ref)
    xn = xn_ref[...]
    gate_ref[...] = jnp.dot(xn, wg_ref[...], preferred_element_type=F32) + bg_ref[...]
    kt_ref[...] = lax.dot_general(wkt_ref[...], xn, (((1,), (1,)), ((), ())),
                                  preferred_element_type=F32).astype(kt_ref.dtype)
    for c, w in _col_chunks(w_ref.shape[1]):
        z_ref[:, c:c + w] = jnp.dot(xn, w_ref[:, c:c + w], preferred_element_type=F32).astype(z_ref.dtype)


def _inproj_mlstm(x, g, w, wkt, wg, bg):
    t, d = x.shape
    n = w.shape[1]
    qk = wkt.shape[0]
    tm = _pick(t, (512, 256, 128))
    return pl.pallas_call(
        _inproj_mlstm_kernel,
        grid=(t // tm,),
        in_specs=[
            pl.BlockSpec((tm, d), lambda i: (i, 0)),
            _resident((1, d)),
            _resident((d, n)),
            _resident((qk, d)),
            _resident((d, N_GATE_LANES)),
            _resident((1, N_GATE_LANES)),
        ],
        out_specs=[
            pl.BlockSpec((tm, n), lambda i: (i, 0)),
            pl.BlockSpec((qk, tm), lambda i: (0, i)),
            pl.BlockSpec((tm, N_GATE_LANES), lambda i: (i, 0)),
        ],
        out_shape=[
            jax.ShapeDtypeStruct((t, n), BF16),
            jax.ShapeDtypeStruct((qk, t), BF16),
            jax.ShapeDtypeStruct((t, N_GATE_LANES), F32),
        ],
        scratch_shapes=[pltpu.VMEM((tm, d), BF16)],
        compiler_params=_cparams("parallel"),
        name="inproj_mlstm",
    )(x, g, w, wkt, wg, bg)


def _mlstm_kernel(*refs, reverse, fuse, heads, dk, dv, chunks):
    n_in = 5 if fuse else 4
    L = MLSTM_CHUNK
    state = refs[n_in + 1:]

    @pl.when(pl.program_id(1) == 0)
    def _():
        for ref in state:
            ref[...] = jnp.zeros_like(ref)

    for c in (range(chunks - 1, -1, -1) if reverse else range(chunks)):
        rows = pl.ds(c * L, L)
        views = []
        for idx, ref in enumerate(refs[:n_in + 1]):
            if idx == 1:
                views.append(ref.at[:, rows])
            elif fuse and idx == 4:
                views.append(ref)
            else:
                views.append(ref.at[rows])
        _mlstm_chunk(*views, *state, reverse=reverse, fuse=fuse, heads=heads, dk=dk, dv=dv)


def _mlstm_chunk(*refs, reverse, fuse, heads, dk, dv):
    vdim = heads * dv
    if fuse:
        n_in = 5
        z_ref, kt_ref, gate_ref, hf_ref, gh_ref = refs[:n_in]
        v_ref, v0, q_ref, q0 = z_ref, 0, z_ref, 2 * vdim
    else:
        n_in = 4
        q_ref, kt_ref, v_ref, gate_ref = refs[:n_in]
        v0, q0 = 0, 0
    out_ref = refs[n_in]
    state = refs[n_in + 1:]
    c_refs, m_refs = state[0::2], state[1::2]
    L = MLSTM_CHUNK
    ng = 4 * heads

    gc = _soft_cap(gate_ref[...])
    lsc = _log_sigmoid(gc)
    row = lax.broadcasted_iota(I32, (L, L), 0)
    col = lax.broadcasted_iota(I32, (L, L), 1)
    mask_t = col >= row if reverse else col <= row
    incl_t = jnp.where(mask_t, 1.0, 0.0).astype(BF16)
    cc = jnp.dot(incl_t, jnp.concatenate(_split3(lsc), axis=1), preferred_element_type=F32)
    cum_cols = cc[:, :LANES] + cc[:, LANES:2 * LANES] + cc[:, 2 * LANES:]
    gr = gc.T[:ng, :]
    cum_rows = cum_cols.T[:ng, :]
    end = 0 if reverse else L - 1
    scale = dk ** -0.5
    reps = dv // LANES

    def lanes(a, n):
        return jnp.concatenate([a] * n, axis=1) if n > 1 else a

    def first(h):
        ci = (2 * heads if reverse else 0) + h
        cf = ci + heads
        i_col = jnp.broadcast_to(gc[:, ci:ci + 1], (L, LANES))
        b_col = jnp.broadcast_to(cum_cols[:, cf:cf + 1], (L, LANES))
        i_row = gr[ci:ci + 1, :]
        b_row = cum_rows[cf:cf + 1, :]
        g = b_col[end:end + 1, :]
        m_prev = m_refs[h][...]
        a_col = g - b_col + i_col
        m_new = jnp.maximum(g + m_prev, jnp.max(a_col, axis=0, keepdims=True))
        decay = jnp.exp(g + m_prev - m_new)
        w_row = jnp.exp(g - b_row + i_row - m_new)

        q = q_ref[:, q0 + h * dk:q0 + (h + 1) * dk]
        kt = kt_ref[h * dk:(h + 1) * dk, :]
        qs_b = (q.astype(F32) * scale).astype(BF16)
        cx_old = c_refs[h][...]
        num_x = jnp.dot(qs_b, cx_old.astype(BF16), preferred_element_type=F32)
        qk = jnp.dot(qs_b, kt, preferred_element_type=F32)

        log_inter = b_col + m_prev
        dlog = jnp.where(mask_t, b_col - b_row + i_row, -jnp.inf)
        m_t = jnp.maximum(log_inter, jnp.max(dlog, axis=1, keepdims=True))
        return m_new, decay, w_row, kt, cx_old, num_x, qk, log_inter, dlog, m_t

    ones = jnp.ones((L, LANES), BF16)

    def second(h, staged):
        m_new, decay, w_row, kt, cx_old, num_x, qk, log_inter, dlog, m_t = staged
        vx = jnp.concatenate([v_ref[:, v0 + h * dv:v0 + (h + 1) * dv], ones], axis=1)
        s = qk * jnp.exp(dlog - m_t)
        inter = jnp.exp(log_inter - m_t)
        sv = jnp.dot(s.astype(BF16), vx, preferred_element_type=F32)
        num = lanes(inter, reps) * num_x[:, :dv] + sv[:, :dv]
        den = inter * num_x[:, dv:] + sv[:, dv:]
        hh = num * lanes(1.0 / jnp.maximum(jnp.abs(den), jnp.exp(-m_t)), reps)

        kwt = (kt.astype(F32) * w_row).astype(BF16)
        c_refs[h][...] = lanes(decay, reps + 1) * cx_old + jnp.dot(kwt, vx, preferred_element_type=F32)
        m_refs[h][...] = m_new

        hs = slice(h * dv, (h + 1) * dv)
        if fuse:
            tot = hf_ref[:, hs] + hh
            ms = jnp.mean(tot * tot, axis=-1, keepdims=True)
            o = z_ref[:, vdim + h * dv:vdim + (h + 1) * dv].astype(F32)
            gated = tot * lax.rsqrt(ms + EPS) * gh_ref[:, hs] * _sigmoid(o)
            out_ref[:, hs] = gated.astype(out_ref.dtype)
        else:
            out_ref[:, hs] = hh

    for h0 in range(0, heads, HEAD_GROUP):
        group = range(h0, min(h0 + HEAD_GROUP, heads))
        staged = [first(h) for h in group]
        for h, st in zip(group, staged):
            second(h, st)


def _mlstm(z, kt, gates, *, reverse, fused=None):
    b, s, width = z.shape
    heads = MLSTM_HEADS
    qk = kt.shape[0]
    vdim = (width - qk) // 2
    dk, dv = qk // heads, vdim // heads
    chunks = _pick(s // MLSTM_CHUNK, (MLSTM_CHUNKS_PER_STEP, 1))
    L = MLSTM_CHUNK * chunks
    nc = s // L
    assert MLSTM_CHUNK == N_GATE_LANES and dv % LANES == 0 and (2 * vdim) % qk == 0

    def ch(c):
        return nc - 1 - c if reverse else c

    kt_spec = pl.BlockSpec((qk, L), lambda bi, c: (0, bi * nc + ch(c)))
    gate_spec = pl.BlockSpec((None, L, N_GATE_LANES), lambda bi, c: (bi, ch(c), 0))
    if fused is None:
        in_specs = [
            pl.BlockSpec((None, L, qk), lambda bi, c: (bi, ch(c), (2 * vdim) // qk)),
            kt_spec,
            pl.BlockSpec((None, L, vdim), lambda bi, c: (bi, ch(c), 0)),
            gate_spec,
        ]
        args = [z, kt, z, gates]
    else:
        hf, gh = fused
        in_specs = [
            pl.BlockSpec((None, L, width), lambda bi, c: (bi, ch(c), 0)),
            kt_spec,
            gate_spec,
            pl.BlockSpec((None, L, vdim), lambda bi, c: (bi, ch(c), 0)),
            pl.BlockSpec((1, vdim), lambda bi, c: (0, 0)),
        ]
        args = [z, kt, gates, hf, gh]
    return pl.pallas_call(
        functools.partial(_mlstm_kernel, reverse=reverse, fuse=fused is not None,
                          heads=heads, dk=dk, dv=dv, chunks=chunks),
        grid=(b, nc),
        in_specs=in_specs,
        out_specs=pl.BlockSpec((None, L, vdim), lambda bi, c: (bi, ch(c), 0)),
        out_shape=jax.ShapeDtypeStruct((b, s, vdim), BF16 if fused is not None else F32),
        scratch_shapes=[pltpu.VMEM((dk, dv + LANES), F32), pltpu.VMEM((1, LANES), F32)] * heads,
        compiler_params=_cparams("parallel", "arbitrary"),
        name="mlstm_bwd" if reverse else "mlstm_fwd",
    )(*args)


def _outproj_kernel(h_ref, w_ref, x_ref, o_ref):
    h = h_ref[...]
    for c, w in _col_chunks(w_ref.shape[1]):
        o_ref[:, c:c + w] = x_ref[:, c:c + w] + jnp.dot(h, w_ref[:, c:c + w], preferred_element_type=F32)


def _outproj(h, w, x):
    t, kdim = h.shape
    n = w.shape[1]
    tm = _pick(t, (512, 256, 128))
    return pl.pallas_call(
        _outproj_kernel,
        grid=(t // tm,),
        in_specs=[
            pl.BlockSpec((tm, kdim), lambda i: (i, 0)),
            _resident((kdim, n)),
            pl.BlockSpec((tm, n), lambda i: (i, 0)),
        ],
        out_specs=pl.BlockSpec((tm, n), lambda i: (i, 0)),
        out_shape=jax.ShapeDtypeStruct((t, n), F32),
        compiler_params=_cparams("parallel"),
        name="outproj",
    )(h, w, x)


def _inproj_sgu_kernel(x_ref, g_ref, w_ref, z_ref, ssq_ref, xn_ref):
    _rmsnorm_rows(x_ref, g_ref, xn_ref)
    xn = xn_ref[...]
    n = w_ref.shape[1]
    part = None
    for c, w in _col_chunks(n // 2) + [(n // 2 + c, w) for c, w in _col_chunks(n // 2)]:
        z = jax.nn.gelu(jnp.dot(xn, w_ref[:, c:c + w], preferred_element_type=F32))
        z_ref[:, c:c + w] = z.astype(z_ref.dtype)
        if c >= n // 2:
            z2 = z * z
            for l in range(0, w, LANES):
                part = z2[:, l:l + LANES] if part is None else part + z2[:, l:l + LANES]
    ssq_ref[...] = part


def _inproj_sgu(x, g, w):
    t, d = x.shape
    n = w.shape[1]
    tm = _pick(t, (512, 256, 128))
    return pl.pallas_call(
        _inproj_sgu_kernel,
        grid=(t // tm,),
        in_specs=[
            pl.BlockSpec((tm, d), lambda i: (i, 0)),
            _resident((1, d)),
            _resident((d, n)),
        ],
        out_specs=[
            pl.BlockSpec((tm, n), lambda i: (i, 0)),
            pl.BlockSpec((tm, LANES), lambda i: (i, 0)),
        ],
        out_shape=[
            jax.ShapeDtypeStruct((t, n), BF16),
            jax.ShapeDtypeStruct((t, LANES), F32),
        ],
        scratch_shapes=[pltpu.VMEM((tm, d), BF16)],
        compiler_params=_cparams("parallel"),
        name="inproj_sgu",
    )(x, g, w)


def _sgu_mix_kernel(u_ref, v_ref, ssq_ref, gv_ref, ws_ref, bs_ref, wo_ref, x_ref, o_ref, gated_ref,
                    *, groups, gd):
    rows = u_ref.shape[0]
    width = v_ref.shape[1]
    gv = gv_ref[...]
    for c in range(rows // SGU_CHUNK):
        rs = slice(c * SGU_CHUNK, (c + 1) * SGU_CHUNK)
        ms = jnp.sum(ssq_ref[rs, :], axis=-1, keepdims=True) / width
        vn = (v_ref[rs, :].astype(F32) * lax.rsqrt(ms + EPS) * gv).astype(BF16)
        for gi in range(groups):
            cs = slice(gi * gd, (gi + 1) * gd)
            mixed = jnp.dot(ws_ref[gi], vn[:, cs], preferred_element_type=F32)
            bias = bs_ref[gi]
            bias = jnp.concatenate([bias] * (gd // LANES), axis=1) if gd > LANES else bias
            gated_ref[rs, cs] = (u_ref[rs, cs].astype(F32) * (mixed + bias)).astype(BF16)
    gated = gated_ref[...]
    for c, w in _col_chunks(wo_ref.shape[1]):
        o_ref[:, c:c + w] = x_ref[:, c:c + w] + jnp.dot(gated, wo_ref[:, c:c + w], preferred_element_type=F32)


def _sgu_mix(z, ssq, gv, ws, bs, wo, x):
    t, d = x.shape
    width = z.shape[1] // 2
    groups = ws.shape[0]
    tm = _pick(t, (512, 256, 128))
    return pl.pallas_call(
        functools.partial(_sgu_mix_kernel, groups=groups, gd=width // groups),
        grid=(t // tm,),
        in_specs=[
            pl.BlockSpec((tm, width), lambda i: (i, 0)),
            pl.BlockSpec((tm, width), lambda i: (i, 1)),
            pl.BlockSpec((tm, LANES), lambda i: (i, 0)),
            _resident((1, width)),
            _resident((groups, SGU_CHUNK, SGU_CHUNK)),
            _resident((groups, SGU_CHUNK, LANES)),
            _resident((width, d)),
            pl.BlockSpec((tm, d), lambda i: (i, 0)),
        ],
        out_specs=pl.BlockSpec((tm, d), lambda i: (i, 0)),
        out_shape=jax.ShapeDtypeStruct((t, d), F32),
        scratch_shapes=[pltpu.VMEM((tm, width), BF16)],
        compiler_params=_cparams("parallel"),
        name="sgu_mix",
    )(z, z, ssq, gv, ws, bs, wo, x)


def _router_kernel(x_ref, g_ref, w_ref, b_ref, eid_ref, rank_ref, gate_ref, cnt_ref, hn_ref, carry_ref,
                   *, n_grp, epg, rows):
    tm = x_ref.shape[0]

    @pl.when(pl.program_id(0) == 0)
    def _():
        carry_ref[...] = jnp.zeros_like(carry_ref)

    _rmsnorm_rows(x_ref, g_ref, hn_ref)
    logits = jnp.dot(hn_ref[...], w_ref[...], preferred_element_type=F32) + b_ref[...]
    lt = logits.T[:rows, :]
    ridx = lax.broadcasted_iota(I32, (rows, tm), 0).astype(F32)
    big = float(rows)

    gl = jnp.where(ridx < n_grp, lt, -jnp.inf)
    gmax = jnp.max(gl, axis=0, keepdims=True)
    grp = jnp.min(jnp.where(gl == gmax, ridx, big), axis=0, keepdims=True)
    grp_p = 1.0 / jnp.sum(jnp.exp(gl - gmax), axis=0, keepdims=True)

    lo = n_grp + grp * epg
    el = jnp.where(ridx >= lo, jnp.where(ridx < lo + epg, lt, -jnp.inf), -jnp.inf)
    emax = jnp.max(el, axis=0, keepdims=True)
    ee = jnp.exp(el - emax)
    p = ee / jnp.sum(ee, axis=0, keepdims=True)
    pm = jnp.where(ridx >= lo, jnp.where(ridx < lo + epg, p, -1.0), -1.0)
    p1 = jnp.max(pm, axis=0, keepdims=True)
    i1 = jnp.min(jnp.where(pm == p1, ridx, big), axis=0, keepdims=True)
    pm2 = jnp.where(ridx == i1, -1.0, pm)
    p2 = jnp.max(pm2, axis=0, keepdims=True)
    i2 = jnp.min(jnp.where(pm2 == p2, ridx, big), axis=0, keepdims=True)
    psum = p1 + p2
    g1 = grp_p * (p1 / psum)
    g2 = grp_p * (p2 / psum)

    hit1 = ridx == i1
    hit2 = ridx == i2
    onehot = jnp.where(hit1, 1.0, jnp.where(hit2, 1.0, 0.0))
    r = lax.broadcasted_iota(I32, (tm, tm), 0)
    c = lax.broadcasted_iota(I32, (tm, tm), 1)
    earlier = jnp.where(r < c, 1.0, 0.0).astype(BF16)
    carry = carry_ref[:, 0:1]
    before = jnp.dot(onehot.astype(BF16), earlier, preferred_element_type=F32) + carry
    rank1 = jnp.sum(jnp.where(hit1, before, 0.0), axis=0, keepdims=True)
    rank2 = jnp.sum(jnp.where(hit2, before, 0.0), axis=0, keepdims=True)
    carry = carry + jnp.sum(onehot, axis=1, keepdims=True)
    carry_ref[...] = jnp.broadcast_to(carry, carry_ref.shape)
    cnt_ref[...] = jnp.broadcast_to(carry, cnt_ref.shape)

    sub = lax.broadcasted_iota(I32, (8, tm), 0)
    eid_ref[...] = jnp.where(sub == 0, i1 - n_grp, jnp.where(sub == 1, i2 - n_grp, 0.0)).astype(I32)
    rank_ref[...] = jnp.where(sub == 0, rank1, jnp.where(sub == 1, rank2, 0.0)).astype(I32)
    sub_l = lax.broadcasted_iota(I32, (LANES, tm), 0)
    gate_ref[...] = jnp.where(sub_l == 0, g1, jnp.where(sub_l == 1, g2, 0.0)).T


def _router(x, g, w, b, *, n_grp, epg):
    t, d = x.shape
    tm = _pick(t, (512, 256, 128))
    rows = -(-(n_grp + n_grp * epg) // 8) * 8
    return pl.pallas_call(
        functools.partial(_router_kernel, n_grp=n_grp, epg=epg, rows=rows),
        grid=(t // tm,),
        in_specs=[
            pl.BlockSpec((tm, d), lambda i: (i, 0)),
            pl.BlockSpec((1, d), lambda i: (0, 0)),
            pl.BlockSpec((d, LANES), lambda i: (0, 0)),
            pl.BlockSpec((1, LANES), lambda i: (0, 0)),
        ],
        out_specs=[
            pl.BlockSpec((8, tm), lambda i: (0, i)),
            pl.BlockSpec((8, tm), lambda i: (0, i)),
            pl.BlockSpec((tm, LANES), lambda i: (i, 0)),
            pl.BlockSpec((rows, LANES), lambda i: (0, 0)),
        ],
        out_shape=[
            jax.ShapeDtypeStruct((8, t), I32),
            jax.ShapeDtypeStruct((8, t), I32),
            jax.ShapeDtypeStruct((t, LANES), F32),
            jax.ShapeDtypeStruct((rows, LANES), F32),
        ],
        scratch_shapes=[pltpu.VMEM((tm, d), BF16), pltpu.VMEM((rows, LANES), F32)],
        compiler_params=_cparams("arbitrary"),
        name="router",
    )(x, g, w, b)


def _slots_kernel(ps_ref, eid_ref, rank_ref, slot_ref, *, n_exp):
    eid = eid_ref[...]
    base = jnp.zeros_like(eid)
    for e in range(n_exp):
        base = jnp.where(eid == e, ps_ref[e], base)
    slot_ref[...] = base + rank_ref[...]


def _slots(pstarts, eid, rank):
    rows, t = eid.shape
    tm = _pick(t, (4096, 2048, 1024, 512, 256, 128))
    return pl.pallas_call(
        functools.partial(_slots_kernel, n_exp=pstarts.shape[0]),
        grid_spec=pltpu.PrefetchScalarGridSpec(
            num_scalar_prefetch=1,
            grid=(t // tm,),
            in_specs=[
                pl.BlockSpec((rows, tm), lambda i, ps: (0, i)),
                pl.BlockSpec((rows, tm), lambda i, ps: (0, i)),
            ],
            out_specs=pl.BlockSpec((rows, tm), lambda i, ps: (0, i)),
        ),
        out_shape=jax.ShapeDtypeStruct((rows, t), I32),
        compiler_params=_cparams("parallel"),
        name="moe_slots",
    )(pstarts, eid, rank)


def _dispatch_kernel(zb_ref, d0_ref, d1_ref, x_ref, g_ref, xb_ref, pk_ref, zero_ref, sem, zsem):
    i = pl.program_id(0)
    last = pl.num_programs(0) - 1
    tm, d = x_ref.shape
    half = d // 2
    rt = half // LANES
    slot = i % 2
    g = g_ref[...]

    @pl.when(i == 0)
    def _():
        zero_ref[...] = jnp.zeros_like(zero_ref)

        def zero_copy(j):
            lines = MOE_BLOCK * rt
            start = pl.multiple_of(jnp.maximum(zb_ref[j], 0) * lines, lines)
            return pltpu.make_async_copy(zero_ref, xb_ref.at[pl.ds(start, lines)], zsem.at[0])

        def zstart(j, c):
            @pl.when(zb_ref[j] >= 0)
            def _():
                zero_copy(j).start()
            return c

        def zwait(j, c):
            @pl.when(zb_ref[j] >= 0)
            def _():
                zero_copy(j).wait()
            return c

        lax.fori_loop(0, zb_ref.shape[0], zstart, 0)
        lax.fori_loop(0, zb_ref.shape[0], zwait, 0)

    def pack(j, c):
        r = pl.multiple_of(j * ROW_CHUNK, ROW_CHUNK)
        x = x_ref[pl.ds(r, ROW_CHUNK), :]
        ms = jnp.mean(x * x, axis=-1, keepdims=True)
        hn = (x * lax.rsqrt(ms + EPS) * g).astype(BF16).astype(F32)
        bits = lax.bitcast_convert_type(hn, U32)
        lo = lax.shift_right_logical(bits[:, :half], jnp.uint32(16))
        hi = bits[:, half:] & jnp.uint32(0xFFFF0000)
        words = hi | lo
        for s in range(rt):
            pk_ref[slot, pl.ds(r * rt + s, ROW_CHUNK, stride=rt), :] = words[:, s * LANES:(s + 1) * LANES]
        return c

    lax.fori_loop(0, tm // ROW_CHUNK, pack, 0)

    def row_copy(r, dest, prio):
        src = pk_ref.at[slot, pl.ds(pl.multiple_of(r * rt, rt), rt)]
        dst = xb_ref.at[pl.ds(pl.multiple_of(dest * rt, rt), rt)]
        pltpu.make_async_copy(src, dst, sem.at[slot]).start(priority=prio)

    def issue(j, c):
        for u in range(DMA_UNROLL):
            r = j * DMA_UNROLL + u
            row_copy(r, d0_ref[r], u % 2)
            row_copy(r, d1_ref[r], (u + 1) % 2)
        return c

    lax.fori_loop(0, tm // DMA_UNROLL, issue, 0)

    def drain(s):
        for _ in range(TOP_K):
            pltpu.make_async_copy(pk_ref.at[s], xb_ref.at[pl.ds(0, tm * rt)], sem.at[s]).wait()

    @pl.when(i > 0)
    def _():
        drain(1 - slot)

    @pl.when(i == last)
    def _():
        drain(slot)


def _dispatch(zero_blocks, d0, d1, x, g, n_rows):
    t, d = x.shape
    tm = _pick(t, (512, 256, 128))
    rt = d // 2 // LANES
    return pl.pallas_call(
        _dispatch_kernel,
        grid_spec=pltpu.PrefetchScalarGridSpec(
            num_scalar_prefetch=1,
            grid=(t // tm,),
            in_specs=[
                pl.BlockSpec((tm,), lambda i, zb: (i,), memory_space=pltpu.SMEM),
                pl.BlockSpec((tm,), lambda i, zb: (i,), memory_space=pltpu.SMEM),
                pl.BlockSpec((tm, d), lambda i, zb: (i, 0)),
                pl.BlockSpec((1, d), lambda i, zb: (0, 0)),
            ],
            out_specs=pl.BlockSpec(memory_space=pl.ANY),
            scratch_shapes=[pltpu.VMEM((2, tm * rt, LANES), U32), pltpu.VMEM((MOE_BLOCK * rt, LANES), U32),
                            pltpu.SemaphoreType.DMA((2,)), pltpu.SemaphoreType.DMA((1,))],
        ),
        out_shape=jax.ShapeDtypeStruct((n_rows * rt, LANES), U32),
        compiler_params=_cparams("arbitrary", disable_bounds_checks=True),
        name="moe_dispatch",
    )(zero_blocks, d0, d1, x, g)


def _cast_rows(src_ref, dst_ref):
    rows = src_ref.shape[0]
    step = _pick(rows, (256, 128))

    def body(j, c):
        r = pl.multiple_of(j * step, step)
        dst_ref[pl.ds(r, step), :] = src_ref[pl.ds(r, step), :].astype(dst_ref.dtype)
        return c

    lax.fori_loop(0, rows // step, body, 0)


def _experts_kernel(be_ref, nu_ref, xb_ref, wg_ref, wu_ref, wd_ref, yb_ref, wgb_ref, wub_ref, wdb_ref):
    s = pl.program_id(0)
    nu = nu_ref[0]
    blk = s - 1
    half = wgb_ref.shape[0] // 2
    rt_in = half // LANES

    @pl.when(jnp.logical_and(s >= 1, blk < nu))
    def _():
        w = jnp.concatenate([xb_ref[pl.ds(t, MOE_BLOCK, stride=rt_in), :] for t in range(rt_in)], axis=1)
        x_lo = lax.bitcast_convert_type(lax.shift_left(w, jnp.uint32(16)), F32).astype(BF16)
        x_hi = lax.bitcast_convert_type(w & jnp.uint32(0xFFFF0000), F32).astype(BF16)
        h1 = (jnp.dot(x_lo, wgb_ref[:half, :], preferred_element_type=F32)
              + jnp.dot(x_hi, wgb_ref[half:, :], preferred_element_type=F32))
        h2 = (jnp.dot(x_lo, wub_ref[:half, :], preferred_element_type=F32)
              + jnp.dot(x_hi, wub_ref[half:, :], preferred_element_type=F32))
        hid = ((h1 * _sigmoid(h1)) * h2).astype(BF16)
        for c, width in _col_chunks(wdb_ref.shape[1]):
            yb_ref[:, c:c + width] = jnp.dot(hid, wdb_ref[:, c:c + width], preferred_element_type=F32)

    @pl.when(jnp.logical_and(s >= 1, blk >= nu))
    def _():
        yb_ref[...] = jnp.zeros_like(yb_ref)

    last_blk = be_ref.shape[0] - 1
    changed = be_ref[jnp.minimum(s, last_blk)] != be_ref[jnp.clip(s - 1, 0, last_blk)]

    @pl.when(jnp.logical_and(s < nu, jnp.logical_or(s == 0, changed)))
    def _():
        _cast_rows(wg_ref, wgb_ref)
        _cast_rows(wu_ref, wub_ref)
        _cast_rows(wd_ref, wdb_ref)


def _experts(blk_exp, n_used, xb, wg, wu, wd, layer):
    d, de = wg.shape[-2:]
    rt_in = d // 2 // LANES
    n_blk = xb.shape[0] // (MOE_BLOCK * rt_in)

    def weight(s, be, nu):
        return (layer, be[jnp.minimum(s, nu[0] - 1)], 0, 0)

    return pl.pallas_call(
        _experts_kernel,
        grid_spec=pltpu.PrefetchScalarGridSpec(
            num_scalar_prefetch=2,
            grid=(n_blk + 1,),
            in_specs=[
                pl.BlockSpec((MOE_BLOCK * rt_in, LANES),
                             lambda s, be, nu: (jnp.clip(s - 1, 0, nu[0] - 1), 0)),
                pl.BlockSpec((None, None, d, de), weight),
                pl.BlockSpec((None, None, d, de), weight),
                pl.BlockSpec((None, None, de, d), weight),
            ],
            out_specs=pl.BlockSpec((MOE_BLOCK, d), lambda s, be, nu: (jnp.maximum(s - 1, 0), 0)),
            scratch_shapes=[pltpu.VMEM((d, de), BF16), pltpu.VMEM((d, de), BF16),
                            pltpu.VMEM((de, d), BF16)],
        ),
        out_shape=jax.ShapeDtypeStruct((n_blk * MOE_BLOCK, d), F32),
        compiler_params=_cparams("arbitrary"),
        name="moe_experts",
    )(blk_exp, n_used, xb, wg, wu, wd)


def _combine_kernel(*refs, final):
    if final:
        d0_ref, d1_ref, x_ref, gate_ref, yb_ref, gf_ref, o_ref, a_ref, b_ref, sem = refs
    else:
        d0_ref, d1_ref, x_ref, gate_ref, yb_ref, o_ref, a_ref, b_ref, sem = refs
    i = pl.program_id(0)
    n_tiles = pl.num_programs(0) - 1
    tm = x_ref.shape[0]
    slot = i % 2

    @pl.when(i < n_tiles)
    def _():
        def issue(j, c):
            for u in range(DMA_UNROLL):
                r = j * DMA_UNROLL + u
                pltpu.make_async_copy(yb_ref.at[pl.ds(d0_ref[r], 1)], a_ref.at[slot, pl.ds(r, 1)],
                                      sem.at[0, slot]).start(priority=u % 2)
                pltpu.make_async_copy(yb_ref.at[pl.ds(d1_ref[r], 1)], b_ref.at[slot, pl.ds(r, 1)],
                                      sem.at[1, slot]).start(priority=(u + 1) % 2)
            return c

        lax.fori_loop(0, tm // DMA_UNROLL, issue, 0)

    @pl.when(i > 0)
    def _():
        prev = 1 - slot
        pltpu.make_async_copy(yb_ref.at[pl.ds(0, tm)], a_ref.at[prev], sem.at[0, prev]).wait()
        pltpu.make_async_copy(yb_ref.at[pl.ds(0, tm)], b_ref.at[prev], sem.at[1, prev]).wait()

        def body(j, c):
            r = pl.multiple_of(j * ROW_CHUNK, ROW_CHUNK)
            rs = pl.ds(r, ROW_CHUNK)
            gates = gate_ref[rs, :]
            y = gates[:, 0:1] * a_ref[prev, rs, :] + gates[:, 1:2] * b_ref[prev, rs, :]
            xn = x_ref[rs, :] + y
            if final:
                ms = jnp.mean(xn * xn, axis=-1, keepdims=True)
                xn = xn * lax.rsqrt(ms + EPS) * gf_ref[...]
            o_ref[rs, :] = xn
            return c

        lax.fori_loop(0, tm // ROW_CHUNK, body, 0)


def _combine(d0, d1, x, gates, yb, g_final):
    t, d = x.shape
    tm = _pick(t, (256, 128))
    n = t // tm
    final = g_final is not None

    def cur(i):
        return (jnp.minimum(i, n - 1),)

    def behind(i):
        return (jnp.maximum(i - 1, 0), 0)

    in_specs = [
        pl.BlockSpec((tm,), cur, memory_space=pltpu.SMEM),
        pl.BlockSpec((tm,), cur, memory_space=pltpu.SMEM),
        pl.BlockSpec((tm, d), behind),
        pl.BlockSpec((tm, LANES), behind),
        pl.BlockSpec(memory_space=pl.ANY),
    ]
    args = [d0, d1, x, gates, yb]
    if final:
        in_specs.append(pl.BlockSpec((1, d), lambda i: (0, 0)))
        args.append(g_final)
    return pl.pallas_call(
        functools.partial(_combine_kernel, final=final),
        grid=(n + 1,),
        in_specs=in_specs,
        out_specs=pl.BlockSpec((tm, d), behind),
        out_shape=jax.ShapeDtypeStruct((t, d), F32),
        scratch_shapes=[pltpu.VMEM((2, tm, d), F32), pltpu.VMEM((2, tm, d), F32),
                        pltpu.SemaphoreType.DMA((2, 2))],
        compiler_params=_cparams("arbitrary", disable_bounds_checks=True),
        name="moe_combine",
    )(*args)


def _moe(x, g_norm, w_route, b_route, wg, wu, wd, layer, g_final, *, n_grp, epg):
    t, d = x.shape
    n_exp = n_grp * epg
    eid, rank, gates, cnt = _router(x, g_norm, w_route, b_route, n_grp=n_grp, epg=epg)

    counts = cnt[n_grp:n_grp + n_exp, 0].astype(I32)
    pblocks = (counts + MOE_BLOCK - 1) // MOE_BLOCK
    bends = jnp.cumsum(pblocks)
    pstarts = (bends - pblocks) * MOE_BLOCK
    n_blk = (t * TOP_K) // MOE_BLOCK + n_exp
    blk = jnp.arange(n_blk, dtype=I32)
    blk_exp = jnp.minimum(jnp.sum(bends[None, :] <= blk[:, None], axis=1), n_exp - 1).astype(I32)
    n_used = bends[-1:].astype(I32)
    last_blk = jnp.where(counts > 0, bends - 1, -1)
    trail = n_used + jnp.arange(n_exp, dtype=I32)
    zero_blocks = jnp.concatenate([last_blk, jnp.where(trail < n_blk, trail, -1)]).astype(I32)

    slots = _slots(pstarts.astype(I32), eid, rank)
    d0, d1 = slots[0], slots[1]
    xb = _dispatch(zero_blocks, d0, d1, x, g_norm, n_blk * MOE_BLOCK)
    yb = _experts(blk_exp, n_used, xb, wg, wu, wd, layer)
    return _combine(d0, d1, x, gates, yb, g_final)


def _row(v):
    return v.reshape(1, -1).astype(F32)


def _trunk(x, p):
    b, s, d = x.shape
    t = b * s
    depth = p["norm_mix"].shape[0]
    xt = x.reshape(t, d)
    for i in range(depth):
        j = i // 2
        if i % 2 == 0:
            z, kt, gates = _inproj_mlstm(xt, _row(p["norm_mix"][i]), p["mlstm_w_main"][j],
                                         p["mlstm_w_kt"][j], p["mlstm_w_gate"][j], p["mlstm_b_row"][j])
            g3 = gates.reshape(b, s, N_GATE_LANES)
            z3 = z.reshape(b, s, -1)
            hf = _mlstm(z3, kt, g3, reverse=False)
            hn = _mlstm(z3, kt, g3, reverse=True, fused=(hf, _row(p["mlstm_g_hnorm"][j])))
            xt = _outproj(hn.reshape(t, -1), p["mlstm_w_out"][j], xt)
        else:
            z, ssq = _inproj_sgu(xt, _row(p["norm_mix"][i]), p["sgu_w_in"][j])
            xt = _sgu_mix(z, ssq, _row(p["sgu_g_v"][j]), p["sgu_w_s"][j], p["sgu_b_s"][j],
                          p["sgu_w_out"][j], xt)
        g_final = _row(p["norm_final"]) if i == depth - 1 else None
        xt = _moe(xt, _row(p["norm_ffn"][i]), p["moe_w_route"][i], p["moe_b_route"][i],
                  p["moe_w_gate"], p["moe_w_up"], p["moe_w_down"], i, g_final,
                  n_grp=p["n_grp"], epg=p["epg"])
    if depth == 0:
        raise ValueError("depth 0 is not supported")
    return xt.reshape(b, s, d)


def kernel(x_prompt, x_sample, norm_mix, norm_ffn, norm_final, mlstm_w_in, mlstm_b_gates, mlstm_g_hnorm, mlstm_w_out, sgu_w_in, sgu_g_v, sgu_w_s, sgu_b_s, sgu_w_out, moe_w_grp, moe_b_grp, moe_w_exp, moe_b_exp, moe_w_gate, moe_w_up, moe_w_down):
    ng = 4 * MLSTM_HEADS
    n_main = mlstm_w_in.shape[-1] - ng
    qk_dim = (n_main - 2 * mlstm_w_out.shape[-2]) // 2
    n_grp = moe_w_grp.shape[-1]
    n_exp = moe_w_exp.shape[-1]
    assert n_grp + n_exp <= LANES and ng <= N_GATE_LANES

    def pad_lanes(a, width):
        return jnp.pad(a, [(0, 0)] * (a.ndim - 1) + [(0, width - a.shape[-1])])

    p = dict(
        norm_mix=norm_mix, norm_ffn=norm_ffn, norm_final=norm_final,
        mlstm_w_main=jnp.concatenate([mlstm_w_in[..., 2 * qk_dim:n_main], mlstm_w_in[..., :qk_dim]],
                                     axis=-1).astype(BF16),
        mlstm_w_kt=jnp.swapaxes(mlstm_w_in[..., qk_dim:2 * qk_dim], -1, -2).astype(BF16),
        mlstm_w_gate=pad_lanes(mlstm_w_in[..., n_main:], N_GATE_LANES).astype(BF16),
        mlstm_b_row=pad_lanes(mlstm_b_gates, N_GATE_LANES)[:, None, :].astype(F32),
        mlstm_g_hnorm=mlstm_g_hnorm,
        mlstm_w_out=mlstm_w_out.astype(BF16),
        sgu_w_in=sgu_w_in.astype(BF16),
        sgu_g_v=sgu_g_v,
        sgu_w_s=sgu_w_s.astype(BF16),
        sgu_b_s=jnp.broadcast_to(sgu_b_s[..., None], sgu_b_s.shape + (LANES,)).astype(F32),
        sgu_w_out=sgu_w_out.astype(BF16),
        moe_w_route=pad_lanes(jnp.concatenate([moe_w_grp, moe_w_exp], axis=-1), LANES).astype(BF16),
        moe_b_route=pad_lanes(jnp.concatenate([moe_b_grp, moe_b_exp], axis=-1), LANES)[:, None, :].astype(F32),
        moe_w_gate=moe_w_gate, moe_w_up=moe_w_up, moe_w_down=moe_w_down,
        n_grp=n_grp, epg=n_exp // n_grp,
    )
    return (_trunk(x_prompt, p), _trunk(x_sample, p))
```
